```python
import math
import jax, jax.numpy as jnp
from jax import lax
import numpy as np

D_MODEL = 4096
BATCH = 1
SEQ = 8192
DEPTH = 4
DEC_BATCH = 2
DEC_SEQ = 8192
PAST_LEN = 128

GRID_W = 64
Q_BLOCK = 128
EPS = 1e-6
ROPE_THETA = 10000.0
D_FF = -(-8 * D_MODEL // (3 * 256)) * 256
N_EVEN = (DEPTH + 1) // 2
N_ODD = DEPTH // 2

POOL_WINDOWS = (2, 4, 8, 16)
POOL_WIDTH = D_MODEL // 2
POOL_GROUP = POOL_WIDTH // len(POOL_WINDOWS)
MLA_NOPE = 128
MLA_ROPE = 64
MLA_V = 128
MLA_HEADS = (D_MODEL - POOL_WIDTH) // MLA_V
MLA_Q_RANK = D_MODEL // 4
MLA_KV_RANK = D_MODEL // 8
EVEN_IN = POOL_WIDTH + MLA_Q_RANK + MLA_KV_RANK + MLA_ROPE
GQA_HEAD_DIM = 128
FNET_WIDTH = D_MODEL // 4
FNET_GROUPS = 4
FNET_GROUP = FNET_WIDTH // FNET_GROUPS
GQA_Q_HEADS = (D_MODEL - FNET_WIDTH) // GQA_HEAD_DIM
GQA_KV_HEADS = GQA_Q_HEADS // 4
ODD_IN = (GQA_Q_HEADS + 2 * GQA_KV_HEADS) * GQA_HEAD_DIM + FNET_WIDTH

kernel_name = "hybrid_pool_mla_gqa_fnet_encoder"


def rmsnorm(x, w):
    xf = x.astype(jnp.float32)
    y = xf * lax.rsqrt(jnp.mean(xf * xf, axis=-1, keepdims=True) + EPS)
    return (y * w.astype(jnp.float32)).astype(x.dtype)


def rope_angles(pos, dim):
    inv = ROPE_THETA ** (-jnp.arange(0, dim, 2, dtype=jnp.float32) / dim)
    return pos.astype(jnp.float32)[:, None] * inv[None, :]


def apply_rope(x, ang):
    half = x.shape[-1] // 2
    cos = jnp.cos(ang)[None, :, None, :]
    sin = jnp.sin(ang)[None, :, None, :]
    xf = x.astype(jnp.float32)
    x1, x2 = xf[..., :half], xf[..., half:]
    return jnp.concatenate([x1 * cos - x2 * sin, x1 * sin + x2 * cos], axis=-1).astype(x.dtype)


def axial_rope(x, ang_row, ang_col):
    half = x.shape[-1] // 2
    return jnp.concatenate([apply_rope(x[..., :half], ang_row), apply_rope(x[..., half:], ang_col)], axis=-1)


def blocked_attention(q, k, v, scale):
    B, S, Hq, dq = q.shape
    Hk, dv = k.shape[2], v.shape[-1]
    G = Hq // Hk
    nb = S // Q_BLOCK
    qb = q.reshape(B, nb, Q_BLOCK, Hk, G, dq).transpose(1, 0, 2, 3, 4, 5)

    def one_block(qblk):
        s = jnp.einsum('bqkgd,bskd->bkgqs', qblk, k, preferred_element_type=jnp.float32) * scale
        p = jax.nn.softmax(s, axis=-1)
        return jnp.einsum('bkgqs,bskd->bqkgd', p.astype(v.dtype), v)

    o = lax.map(one_block, qb)
    return o.transpose(1, 0, 2, 3, 4, 5).reshape(B, S, Hq * dv)


def pool_mix(u, pool_w, pool_scale):
    B, S, _ = u.shape
    ug = u.reshape(B, S, len(POOL_WINDOWS), POOL_GROUP)
    cs = jnp.pad(jnp.cumsum(ug.astype(jnp.float32), axis=1), ((0, 0), (1, 0), (0, 0), (0, 0)))
    t = jnp.arange(S)
    pooled = []
    for gi, w in enumerate(POOL_WINDOWS):
        lo = jnp.clip(t - w // 2, 0, S)
        hi = jnp.clip(t + w - w // 2, 0, S)
        win_sum = jnp.take(cs[:, :, gi], hi, axis=1) - jnp.take(cs[:, :, gi], lo, axis=1)
        pooled.append(win_sum / (hi - lo).astype(jnp.float32)[None, :, None])
    diff = jnp.stack(pooled, axis=2).astype(u.dtype) - ug
    y = jnp.einsum('bsgc,gcd->bsgd', diff, pool_w) * pool_scale.reshape(len(POOL_WINDOWS), POOL_GROUP)
    return y.reshape(B, S, POOL_WIDTH)


def mla(u_q, u_kv, u_kr, q_norm, wq_b, kv_norm, wkv_b, ang):
    B, S, _ = u_q.shape
    q = (rmsnorm(u_q, q_norm) @ wq_b).reshape(B, S, MLA_HEADS, MLA_NOPE + MLA_ROPE)
    q = jnp.concatenate([q[..., :MLA_NOPE], apply_rope(q[..., MLA_NOPE:], ang)], axis=-1)
    kv = (rmsnorm(u_kv, kv_norm) @ wkv_b).reshape(B, S, MLA_HEADS, MLA_NOPE + MLA_V)
    k_nope, v = kv[..., :MLA_NOPE], kv[..., MLA_NOPE:]
    k_rope = apply_rope(u_kr[:, :, None, :], ang)
    k = jnp.concatenate([k_nope, jnp.broadcast_to(k_rope, (B, S, MLA_HEADS, MLA_ROPE))], axis=-1)
    return blocked_attention(q, k, v, (MLA_NOPE + MLA_ROPE) ** -0.5)


def fourier_mix(u, fnet_w):
    B, S, _ = u.shape
    ug = u.reshape(B, S, FNET_GROUPS, FNET_GROUP).astype(jnp.float32)
    f = jnp.fft.fft2(ug, axes=(1, 3), norm='ortho').real
    return f.reshape(B, S, FNET_WIDTH).astype(u.dtype) @ fnet_w


def even_mixer(h, w_in, pool_w, pool_scale, q_norm, wq_b, kv_norm, wkv_b, w_out, ang):
    z = h @ w_in
    o1 = POOL_WIDTH
    o2 = o1 + MLA_Q_RANK
    o3 = o2 + MLA_KV_RANK
    a = pool_mix(z[..., :o1], pool_w, pool_scale)
    b = mla(z[..., o1:o2], z[..., o2:o3], z[..., o3:], q_norm, wq_b, kv_norm, wkv_b, ang)
    return jnp.concatenate([a, b], axis=-1) @ w_out


def odd_mixer(h, w_in, q_norm, k_norm, fnet_w, w_out, ang_row, ang_col):
    B, S, _ = h.shape
    z = h @ w_in
    qd = GQA_Q_HEADS * GQA_HEAD_DIM
    kd = GQA_KV_HEADS * GQA_HEAD_DIM
    q = z[..., :qd].reshape(B, S, GQA_Q_HEADS, GQA_HEAD_DIM)
    k = z[..., qd:qd + kd].reshape(B, S, GQA_KV_HEADS, GQA_HEAD_DIM)
    v = z[..., qd + kd:qd + 2 * kd].reshape(B, S, GQA_KV_HEADS, GQA_HEAD_DIM)
    q = axial_rope(rmsnorm(q, q_norm), ang_row, ang_col)
    k = axial_rope(rmsnorm(k, k_norm), ang_row, ang_col)
    attn = blocked_attention(q, k, v, GQA_HEAD_DIM ** -0.5)
    f = fourier_mix(z[..., qd + 2 * kd:], fnet_w)
    return jnp.concatenate([attn, f], axis=-1) @ w_out


def swiglu(h, w_gate, w_up, w_down):
    return (jax.nn.silu(h @ w_gate) * (h @ w_up)) @ w_down


def trunk(x, c, mod_w, mod_b, norm_mix, norm_ffn,
          e_w_in, e_pool_w, e_pool_scale, e_q_norm, e_wq_b, e_kv_norm, e_wkv_b, e_w_out,
          o_w_in, o_q_norm, o_k_norm, o_fnet_w, o_w_out,
          ffn_gate, ffn_up, ffn_down, final_norm):
    S = x.shape[1]
    n_rows = S // GRID_W
    t = jnp.arange(S)
    row = jnp.repeat(jnp.arange(n_rows), GRID_W)
    col = jnp.tile(jnp.arange(GRID_W), n_rows)
    ang_1d = rope_angles(t, MLA_ROPE)
    ang_row = rope_angles(row, GQA_HEAD_DIM // 2)
    ang_col = rope_angles(col, GQA_HEAD_DIM // 2)
    c_act = jax.nn.silu(c)
    for l in range(DEPTH):
        m = c_act @ mod_w[l] + mod_b[l]
        sh1, sc1, g1, sh2, sc2, g2 = jnp.split(m[:, None, :], 6, axis=-1)
        h = rmsnorm(x, norm_mix[l]) * (1 + sc1) + sh1
        i = l // 2
        if l % 2 == 0:
            y = even_mixer(h, e_w_in[i], e_pool_w[i], e_pool_scale[i], e_q_norm[i], e_wq_b[i],
                           e_kv_norm[i], e_wkv_b[i], e_w_out[i], ang_1d)
        else:
            y = odd_mixer(h, o_w_in[i], o_q_norm[i], o_k_norm[i], o_fnet_w[i], o_w_out[i], ang_row, ang_col)
        x = x + g1 * y
        h = rmsnorm(x, norm_ffn[l]) * (1 + sc2) + sh2
        x = x + g2 * swiglu(h, ffn_gate[l], ffn_up[l], ffn_down[l])
    return rmsnorm(x, final_norm)


def setup_inputs(seed: int = 0) -> dict:
    key = jax.random.key(seed)
    ks = iter(jax.random.split(key, 32))
    f32 = jnp.float32

    def nrm(shape, fan_in, gain=1.0):
        return jax.random.normal(next(ks), shape, f32) * (gain * fan_in ** -0.5)

    def gvec(shape):
        return 1.0 + 0.1 * jax.random.normal(next(ks), shape, f32)

    D = D_MODEL
    return {
        "x_prompt": jax.random.normal(next(ks), (BATCH, SEQ, D), f32),
        "x_sample": jax.random.normal(next(ks), (DEC_BATCH, DEC_SEQ, D), f32),
        "c_prompt": jax.random.normal(next(ks), (BATCH, D), f32),
        "c_sample": jax.random.normal(next(ks), (DEC_BATCH, D), f32),
        "mod_w": nrm((DEPTH, D, 6 * D), D, 0.5),
        "mod_b": 0.02 * jax.random.normal(next(ks), (DEPTH, 6 * D), f32),
        "norm_mix": gvec((DEPTH, D)),
        "norm_ffn": gvec((DEPTH, D)),
        "e_w_in": nrm((N_EVEN, D, EVEN_IN), D),
        "e_pool_w": nrm((N_EVEN, len(POOL_WINDOWS), POOL_GROUP, POOL_GROUP), POOL_GROUP),
        "e_pool_scale": gvec((N_EVEN, POOL_WIDTH)),
        "e_q_norm": gvec((N_EVEN, MLA_Q_RANK)),
        "e_wq_b": nrm((N_EVEN, MLA_Q_RANK, MLA_HEADS * (MLA_NOPE + MLA_ROPE)), MLA_Q_RANK),
        "e_kv_norm": gvec((N_EVEN, MLA_KV_RANK)),
        "e_wkv_b": nrm((N_EVEN, MLA_KV_RANK, MLA_HEADS * (MLA_NOPE + MLA_V)), MLA_KV_RANK),
        "e_w_out": nrm((N_EVEN, D, D), D),
        "o_w_in": nrm((N_ODD, D, ODD_IN), D),
        "o_q_norm": gvec((N_ODD, GQA_HEAD_DIM)),
        "o_k_norm": gvec((N_ODD, GQA_HEAD_DIM)),
        "o_fnet_w": nrm((N_ODD, FNET_WIDTH, FNET_WIDTH), FNET_WIDTH),
        "o_w_out": nrm((N_ODD, D, D), D),
        "ffn_gate": nrm((DEPTH, D, D_FF), D),
        "ffn_up": nrm((DEPTH, D, D_FF), D),
        "ffn_down": nrm((DEPTH, D_FF, D), D_FF),
        "final_norm": gvec((D,)),
    }


def reference(x_prompt, x_sample, c_prompt, c_sample, mod_w, mod_b, norm_mix, norm_ffn,
              e_w_in, e_pool_w, e_pool_scale, e_q_norm, e_wq_b, e_kv_norm, e_wkv_b, e_w_out,
              o_w_in, o_q_norm, o_k_norm, o_fnet_w, o_w_out,
              ffn_gate, ffn_up, ffn_down, final_norm):
    y_prompt = trunk(x_prompt, c_prompt, mod_w, mod_b, norm_mix, norm_ffn,
                     e_w_in, e_pool_w, e_pool_scale, e_q_norm, e_wq_b, e_kv_norm, e_wkv_b, e_w_out,
                     o_w_in, o_q_norm, o_k_norm, o_fnet_w, o_w_out,
                     ffn_gate, ffn_up, ffn_down, final_norm)
    y_sample = trunk(x_sample, c_sample, mod_w, mod_b, norm_mix, norm_ffn,
                     e_w_in, e_pool_w, e_pool_scale, e_q_norm, e_wq_b, e_kv_norm, e_wkv_b, e_w_out,
                     o_w_in, o_q_norm, o_k_norm, o_fnet_w, o_w_out,
                     ffn_gate, ffn_up, ffn_down, final_norm)
    return (y_prompt, y_sample)
```

```python
import functools
import math

import jax
import jax.numpy as jnp
from jax import lax
from jax.experimental import pallas as pl
from jax.experimental.pallas import tpu as pltpu

F32 = jnp.float32
BF16 = jnp.bfloat16

EPS = 1e-6
ROPE_THETA = 10000.0
GRID_W = 64
POOL_WINDOWS = (2, 4, 8, 16)
MLA_NOPE = 128
MLA_ROPE = 64
MLA_V = 128
MLA_QK_PAD = 256
GQA_HEAD_DIM = 128
GQA_GROUP = 4
FNET_GROUPS = 4
LOG2E = 1.4426950408889634

LANES = 128
SUBLANES = 8
VMEM_LIMIT_BYTES = 52 * 1024 * 1024

HIGHEST = lax.Precision.HIGHEST


def _pick(n, target, quantum=LANES):
    best = None
    t = quantum
    while t <= min(n, target):
        if n % t == 0:
            best = t
        t += quantum
    if best is None:
        return n
    return best


def _params(*sem):
    return pltpu.CompilerParams(dimension_semantics=sem, vmem_limit_bytes=VMEM_LIMIT_BYTES)


def _rot_pair(x, c, sa, sb):
    return x * c + pltpu.roll(x, LANES - 32, 1) * sa + pltpu.roll(x, 32, 1) * sb


def _mod_kernel(c_ref, w_ref, b_ref, o_ref):
    c = c_ref[...]
    act = (c / (1.0 + jnp.exp(-c))).astype(BF16)
    w = w_ref[0].astype(BF16)
    o_ref[0] = jnp.dot(act, w, preferred_element_type=F32) + b_ref[0]


def _modulation(c_pad, mod_w, mod_b):
    depth, d, n = mod_w.shape
    rows = c_pad.shape[0]
    tn = _pick(n, 512)
    return pl.pallas_call(
        _mod_kernel,
        grid=(depth, n // tn),
        in_specs=[
            pl.BlockSpec((rows, d), lambda l, j: (0, 0)),
            pl.BlockSpec((1, d, tn), lambda l, j: (l, 0, j)),
            pl.BlockSpec((1, 1, tn), lambda l, j: (l, 0, j)),
        ],
        out_specs=pl.BlockSpec((1, rows, tn), lambda l, j: (l, 0, j)),
        out_shape=jax.ShapeDtypeStruct((depth, rows, n), F32),
        compiler_params=_params("parallel", "parallel"),
        name="modulation",
    )(c_pad, mod_w, mod_b.reshape(depth, 1, n))


def _normmod_kernel(x_ref, w_ref, m_ref, o_ref, *, shift_row, scale_row):
    x = x_ref[...]
    y = x * lax.rsqrt(jnp.mean(x * x, axis=-1, keepdims=True) + EPS) * w_ref[...]
    sc = m_ref[0, scale_row:scale_row + 1, :]
    sh = m_ref[0, shift_row:shift_row + 1, :]
    o_ref[...] = (y * (1.0 + sc) + sh).astype(o_ref.dtype)


def _norm_kernel(x_ref, w_ref, o_ref):
    x = x_ref[...]
    y = x * lax.rsqrt(jnp.mean(x * x, axis=-1, keepdims=True) + EPS) * w_ref[...]
    o_ref[...] = y.astype(o_ref.dtype)


def _normmod(x, w, mods, seq, shift_row, scale_row):
    t, d = x.shape
    tm = _pick(seq, 256, SUBLANES)
    per_seq = seq // tm
    return pl.pallas_call(
        functools.partial(_normmod_kernel, shift_row=shift_row, scale_row=scale_row),
        grid=(t // tm,),
        in_specs=[
            pl.BlockSpec((tm, d), lambda i: (i, 0)),
            pl.BlockSpec((1, d), lambda i: (0, 0)),
            pl.BlockSpec((1, 6, d), lambda i: (i // per_seq, 0, 0)),
        ],
        out_specs=pl.BlockSpec((tm, d), lambda i: (i, 0)),
        out_shape=jax.ShapeDtypeStruct((t, d), BF16),
        compiler_params=_params("parallel"),
        name="normmod",
    )(x, w.reshape(1, d), mods)


def _final_norm(x, w):
    t, d = x.shape
    tm = _pick(t, 256, SUBLANES)
    return pl.pallas_call(
        _norm_kernel,
        grid=(t // tm,),
        in_specs=[pl.BlockSpec((tm, d), lambda i: (i, 0)),
                  pl.BlockSpec((1, d), lambda i: (0, 0))],
        out_specs=pl.BlockSpec((tm, d), lambda i: (i, 0)),
        out_shape=jax.ShapeDtypeStruct((t, d), F32),
        compiler_params=_params("parallel"),
        name="final_norm",
    )(x, w.reshape(1, d))


def _mm_plain_kernel(a_ref, w_ref, o_ref):
    o_ref[...] = jnp.dot(a_ref[...], w_ref[...], preferred_element_type=F32).astype(o_ref.dtype)


def _matmul(a, w, out_dtype, tm_target=1024, tn_target=512):
    m, k = a.shape
    n = w.shape[1]
    tm = _pick(m, tm_target, SUBLANES)
    tn = _pick(n, tn_target)
    return pl.pallas_call(
        _mm_plain_kernel,
        grid=(m // tm, n // tn),
        in_specs=[pl.BlockSpec((tm, k), lambda i, j: (i, 0)),
                  pl.BlockSpec((k, tn), lambda i, j: (0, j))],
        out_specs=pl.BlockSpec((tm, tn), lambda i, j: (i, j)),
        out_shape=jax.ShapeDtypeStruct((m, n), out_dtype),
        compiler_params=_params("parallel", "parallel"),
        name="matmul",
    )(a, w)


def _mm_resid_kernel(*refs, n_a, gate_row):
    a_refs = refs[:n_a]
    w_refs = refs[n_a:2 * n_a]
    x_ref, m_ref, o_ref = refs[2 * n_a:]
    acc = jnp.dot(a_refs[0][...], w_refs[0][...], preferred_element_type=F32)
    for a_ref, w_ref in zip(a_refs[1:], w_refs[1:]):
        acc = acc + jnp.dot(a_ref[...], w_ref[...], preferred_element_type=F32)
    o_ref[...] = x_ref[...] + m_ref[0, gate_row:gate_row + 1, :] * acc


def _matmul_resid(a_list, w_list, x, mods, seq, gate_row, tm_target, tn_target):
    t, n = x.shape
    tm = _pick(seq, tm_target, SUBLANES)
    tn = _pick(n, tn_target)
    per_seq = seq // tm
    n_a = len(a_list)
    in_specs = [pl.BlockSpec((tm, a.shape[1]), lambda i, j: (i, 0)) for a in a_list]
    in_specs += [pl.BlockSpec((w.shape[0], tn), lambda i, j: (0, j)) for w in w_list]
    in_specs += [pl.BlockSpec((tm, tn), lambda i, j: (i, j)),
                 pl.BlockSpec((1, 6, tn), lambda i, j: (i // per_seq, 0, j))]
    return pl.pallas_call(
        functools.partial(_mm_resid_kernel, n_a=n_a, gate_row=gate_row),
        grid=(t // tm, n // tn),
        in_specs=in_specs,
        out_specs=pl.BlockSpec((tm, tn), lambda i, j: (i, j)),
        out_shape=jax.ShapeDtypeStruct((t, n), F32),
        compiler_params=_params("parallel", "parallel"),
        name="matmul_resid",
    )(*a_list, *w_list, x, mods)


def _mm_swiglu_kernel(a_ref, wg_ref, wu_ref, o_ref):
    a = a_ref[...]
    g = jnp.dot(a, wg_ref[...], preferred_element_type=F32)
    u = jnp.dot(a, wu_ref[...], preferred_element_type=F32)
    o_ref[...] = (g / (1.0 + jnp.exp(-g)) * u).astype(o_ref.dtype)


def _matmul_swiglu(a, wg, wu, tm_target=2048, tn_target=256):
    m, k = a.shape
    n = wg.shape[1]
    tm = _pick(m, tm_target, SUBLANES)
    tn = _pick(n, tn_target)
    return pl.pallas_call(
        _mm_swiglu_kernel,
        grid=(m // tm, n // tn),
        in_specs=[pl.BlockSpec((tm, k), lambda i, j: (i, 0)),
                  pl.BlockSpec((k, tn), lambda i, j: (0, j)),
                  pl.BlockSpec((k, tn), lambda i, j: (0, j))],
        out_specs=pl.BlockSpec((tm, tn), lambda i, j: (i, j)),
        out_shape=jax.ShapeDtypeStruct((m, n), BF16),
        compiler_params=_params("parallel", "parallel"),
        name="matmul_swiglu",
    )(a, wg, wu)


def _mla_q_kernel(a_ref, nw_ref, w_ref, c_ref, sa_ref, sb_ref, o_ref, *, scale):
    a = a_ref[...]
    nrm = (a * lax.rsqrt(jnp.mean(a * a, axis=-1, keepdims=True) + EPS) * nw_ref[...]).astype(BF16)
    acc = jnp.dot(nrm, w_ref[...], preferred_element_type=F32)
    c, sa, sb = c_ref[...], sa_ref[...], sb_ref[...]
    for h in range(acc.shape[1] // MLA_QK_PAD):
        lo = h * MLA_QK_PAD
        o_ref[:, lo:lo + MLA_NOPE] = (acc[:, lo:lo + MLA_NOPE] * scale).astype(o_ref.dtype)
        r = acc[:, lo + MLA_NOPE:lo + MLA_QK_PAD]
        o_ref[:, lo + MLA_NOPE:lo + MLA_QK_PAD] = (_rot_pair(r, c, sa, sb) * scale).astype(o_ref.dtype)


def _mla_q(z, col_block, q_norm, wq_pad, tabs, seq, scale):
    t = z.shape[0]
    rank, n = wq_pad.shape
    tm = _pick(seq, 512, SUBLANES)
    per_seq = seq // tm
    tab_spec = pl.BlockSpec((tm, LANES), lambda i: (i % per_seq, 0))
    return pl.pallas_call(
        functools.partial(_mla_q_kernel, scale=scale),
        grid=(t // tm,),
        in_specs=[pl.BlockSpec((tm, rank), lambda i: (i, col_block)),
                  pl.BlockSpec((1, rank), lambda i: (0, 0)),
                  pl.BlockSpec((rank, n), lambda i: (0, 0)),
                  tab_spec, tab_spec, tab_spec],
        out_specs=pl.BlockSpec((tm, n), lambda i: (i, 0)),
        out_shape=jax.ShapeDtypeStruct((t, n), BF16),
        compiler_params=_params("parallel"),
        name="mla_q_proj",
    )(z, q_norm.reshape(1, rank), wq_pad, *tabs)


def _mla_kv_kernel(a_ref, kr_ref, nw_ref, w_ref, c_ref, sa_ref, sb_ref, k_ref, v_ref):
    a = a_ref[...]
    nrm = (a * lax.rsqrt(jnp.mean(a * a, axis=-1, keepdims=True) + EPS) * nw_ref[...]).astype(BF16)
    acc = jnp.dot(nrm, w_ref[...], preferred_element_type=F32)
    kr = _rot_pair(kr_ref[...], c_ref[...], sa_ref[...], sb_ref[...]).astype(k_ref.dtype)
    width = MLA_NOPE + MLA_V
    for h in range(acc.shape[1] // width):
        lo = h * width
        k_ref[:, h * MLA_QK_PAD:h * MLA_QK_PAD + MLA_NOPE] = acc[:, lo:lo + MLA_NOPE].astype(k_ref.dtype)
        k_ref[:, h * MLA_QK_PAD + MLA_NOPE:(h + 1) * MLA_QK_PAD] = kr
        v_ref[:, h * MLA_V:(h + 1) * MLA_V] = acc[:, lo + MLA_NOPE:lo + width].astype(v_ref.dtype)


def _mla_kv(z, kv_col_block, kr_col_block, kv_norm, wkv, tabs, seq):
    t = z.shape[0]
    rank, n = wkv.shape
    heads = n // (MLA_NOPE + MLA_V)
    tm = _pick(seq, 512, SUBLANES)
    per_seq = seq // tm
    tab_spec = pl.BlockSpec((tm, LANES), lambda i: (i % per_seq, 0))
    return pl.pallas_call(
        _mla_kv_kernel,
        grid=(t // tm,),
        in_specs=[pl.BlockSpec((tm, rank), lambda i: (i, kv_col_block)),
                  pl.BlockSpec((tm, LANES), lambda i: (i, kr_col_block)),
                  pl.BlockSpec((1, rank), lambda i: (0, 0)),
                  pl.BlockSpec((rank, n), lambda i: (0, 0)),
                  tab_spec, tab_spec, tab_spec],
        out_specs=[pl.BlockSpec((tm, heads * MLA_QK_PAD), lambda i: (i, 0)),
                   pl.BlockSpec((tm, heads * MLA_V), lambda i: (i, 0))],
        out_shape=[jax.ShapeDtypeStruct((t, heads * MLA_QK_PAD), BF16),
                   jax.ShapeDtypeStruct((t, heads * MLA_V), BF16)],
        compiler_params=_params("parallel"),
        name="mla_kv_proj",
    )(z, z, kv_norm.reshape(1, rank), wkv, *tabs)


def _gqa_qk_kernel(a_ref, w_ref, nw_ref, c_ref, sa_ref, sb_ref, o_ref, *, scale):
    acc = jnp.dot(a_ref[...], w_ref[...], preferred_element_type=F32)
    nw, c, sa, sb = nw_ref[...], c_ref[...], sa_ref[...], sb_ref[...]
    for h in range(acc.shape[1] // GQA_HEAD_DIM):
        lo = h * GQA_HEAD_DIM
        x = acc[:, lo:lo + GQA_HEAD_DIM]
        y = x * lax.rsqrt(jnp.mean(x * x, axis=-1, keepdims=True) + EPS) * nw
        o_ref[:, lo:lo + GQA_HEAD_DIM] = (_rot_pair(y, c, sa, sb) * scale).astype(o_ref.dtype)


def _gqa_qk(a, w, head_norm, tabs, seq, scale):
    t, k = a.shape
    n = w.shape[1]
    tm = _pick(seq, 1024, SUBLANES)
    tn = _pick(n, 512)
    per_seq = seq // tm
    tab_spec = pl.BlockSpec((tm, LANES), lambda i, j: (i % per_seq, 0))
    return pl.pallas_call(
        functools.partial(_gqa_qk_kernel, scale=scale),
        grid=(t // tm, n // tn),
        in_specs=[pl.BlockSpec((tm, k), lambda i, j: (i, 0)),
                  pl.BlockSpec((k, tn), lambda i, j: (0, j)),
                  pl.BlockSpec((1, GQA_HEAD_DIM), lambda i, j: (0, 0)),
                  tab_spec, tab_spec, tab_spec],
        out_specs=pl.BlockSpec((tm, tn), lambda i, j: (i, j)),
        out_shape=jax.ShapeDtypeStruct((t, n), BF16),
        compiler_params=_params("parallel", "parallel"),
        name="gqa_qk_proj",
    )(a, w, head_norm.reshape(1, GQA_HEAD_DIM), *tabs)


def _flash_kernel(q_ref, k_ref, v_ref, o_ref, m_scr, l_scr, acc_scr, *, n_heads, dq, dv, shared_kv):
    j = pl.program_id(3)

    @pl.when(j == 0)
    def _():
        m_scr[...] = jnp.full(m_scr.shape, -jnp.inf, F32)
        l_scr[...] = jnp.zeros(l_scr.shape, F32)
        acc_scr[...] = jnp.zeros(acc_scr.shape, F32)

    for h in range(n_heads):
        q = q_ref[:, h * dq:(h + 1) * dq]
        if shared_kv:
            k = k_ref[...]
            v = v_ref[...]
        else:
            k = k_ref[:, h * dq:(h + 1) * dq]
            v = v_ref[:, h * dv:(h + 1) * dv]
        s = lax.dot_general(q, k, (((1,), (1,)), ((), ())), preferred_element_type=F32)
        m_prev = m_scr[h][:, 0:1]
        l_prev = l_scr[h][:, 0:1]
        m_new = jnp.maximum(m_prev, jnp.max(s, axis=-1, keepdims=True))
        alpha = jnp.exp2(m_prev - m_new)
        p = jnp.exp2(s - m_new)
        l_new = alpha * l_prev + jnp.sum(p, axis=-1, keepdims=True)
        acc_scr[h] = alpha * acc_scr[h] + jnp.dot(p.astype(v.dtype), v, preferred_element_type=F32)
        m_scr[h] = jnp.broadcast_to(m_new, m_scr.shape[1:])
        l_scr[h] = jnp.broadcast_to(l_new, l_scr.shape[1:])

    @pl.when(j == pl.num_programs(3) - 1)
    def _():
        for h in range(n_heads):
            o_ref[:, h * dv:(h + 1) * dv] = (acc_scr[h] / l_scr[h][:, 0:1]).astype(o_ref.dtype)


def _flash(q, k, v, seq, n_groups, n_heads, dq, dv, shared_kv, tq_target=512, tk_target=1024):
    t = q.shape[0]
    n_seq = t // seq
    tq = _pick(seq, tq_target, SUBLANES)
    tk = _pick(seq, tk_target, SUBLANES)
    qb, kb = seq // tq, seq // tk
    kw = dq if shared_kv else n_heads * dq
    vw = dv if shared_kv else n_heads * dv
    return pl.pallas_call(
        functools.partial(_flash_kernel, n_heads=n_heads, dq=dq, dv=dv, shared_kv=shared_kv),
        grid=(n_seq, n_groups, qb, kb),
        in_specs=[pl.BlockSpec((tq, n_heads * dq), lambda b, g, i, j: (b * qb + i, g)),
                  pl.BlockSpec((tk, kw), lambda b, g, i, j: (b * kb + j, g)),
                  pl.BlockSpec((tk, vw), lambda b, g, i, j: (b * kb + j, g))],
        out_specs=pl.BlockSpec((tq, n_heads * dv), lambda b, g, i, j: (b * qb + i, g)),
        out_shape=jax.ShapeDtypeStruct((t, n_groups * n_heads * dv), BF16),
        scratch_shapes=[pltpu.VMEM((n_heads, tq, LANES), F32),
                        pltpu.VMEM((n_heads, tq, LANES), F32),
                        pltpu.VMEM((n_heads, tq, dv), F32)],
        compiler_params=_params("parallel", "parallel", "parallel", "arbitrary"),
        name="flash_attention",
    )(q, k, v)


POOL_HALO = 16


def _pool_kernel(prev_ref, cur_ref, next_ref, w_ref, s_ref, o_ref, ext_ref, *, seq, group):
    tm = cur_ref.shape[0]
    ext_ref[0:POOL_HALO, :] = prev_ref[...]
    ext_ref[POOL_HALO:POOL_HALO + tm, :] = cur_ref[...]
    ext_ref[POOL_HALO + tm:, :] = next_ref[...]
    pos = (pl.program_id(0) * tm) % seq + lax.broadcasted_iota(jnp.int32, (tm, 1), 0)
    for gi, win in enumerate(POOL_WINDOWS):
        cols = slice(gi * group, (gi + 1) * group)
        total = jnp.zeros((tm, group), F32)
        count = jnp.zeros((tm, 1), F32)
        for off in range(-(win // 2), win - win // 2):
            valid = jnp.logical_and(pos + off >= 0, pos + off < seq)
            term = ext_ref[POOL_HALO + off:POOL_HALO + off + tm, cols]
            total = total + jnp.where(valid, term, 0.0)
            count = count + valid.astype(F32)
        diff = (total / count - cur_ref[:, cols]).astype(BF16)
        y = jnp.dot(diff, w_ref[gi], preferred_element_type=F32) * s_ref[:, cols]
        o_ref[:, cols] = y.astype(o_ref.dtype)


def _pool(z, pool_w, pool_scale, seq):
    t = z.shape[0]
    n_groups, group, _ = pool_w.shape
    width = n_groups * group
    tm = _pick(seq, 256, POOL_HALO)
    halo_blocks = tm // POOL_HALO
    last_halo = t // POOL_HALO - 1
    return pl.pallas_call(
        functools.partial(_pool_kernel, seq=seq, group=group),
        grid=(t // tm,),
        in_specs=[
            pl.BlockSpec((POOL_HALO, width), lambda i: (jnp.maximum(i * halo_blocks - 1, 0), 0)),
            pl.BlockSpec((tm, width), lambda i: (i, 0)),
            pl.BlockSpec((POOL_HALO, width), lambda i: (jnp.minimum((i + 1) * halo_blocks, last_halo), 0)),
            pl.BlockSpec((n_groups, group, group), lambda i: (0, 0, 0)),
            pl.BlockSpec((1, width), lambda i: (0, 0)),
        ],
        out_specs=pl.BlockSpec((tm, width), lambda i: (i, 0)),
        out_shape=jax.ShapeDtypeStruct((t, width), BF16),
        scratch_shapes=[pltpu.VMEM((tm + 2 * POOL_HALO, width), F32)],
        compiler_params=_params("parallel"),
        name="pool_mixer",
    )(z, z, z, pool_w, pool_scale.reshape(1, width))


FFT_INNER = 128


def _fnet_chan_kernel(u_ref, m_ref, o_ref):
    gw = u_ref.shape[1]
    res = jnp.dot(u_ref[...], m_ref[...], precision=HIGHEST, preferred_element_type=F32)
    o_ref[0, 0] = res[:, :gw]
    o_ref[0, 1] = res[:, gw:]


def _fnet_outer_kernel(v_ref, m_ref, o_ref):
    two, n1, tc = v_ref.shape[1:]
    x = v_ref[0].reshape(two * n1, tc)
    y = jnp.dot(m_ref[...], x, precision=HIGHEST, preferred_element_type=F32)
    o_ref[0] = y.reshape(two, n1, tc)


def _fnet_inner_kernel(a_ref, g_ref, o_ref, *, norm):
    x = a_ref[0, :, 0]
    x = x.reshape(x.shape[0] * x.shape[1], x.shape[2])
    y = jnp.dot(g_ref[0], x, precision=HIGHEST, preferred_element_type=F32)
    o_ref[0] = (y * norm).astype(o_ref.dtype)


def _fnet_tables(seq, gw):
    n1 = seq // FFT_INNER
    ch = jnp.arange(gw, dtype=jnp.int32)
    ang = (2.0 * math.pi / gw) * ((ch[:, None] * ch[None, :]) % gw).astype(F32)
    chan = jnp.concatenate([jnp.cos(ang), -jnp.sin(ang)], axis=1)
    i1 = jnp.arange(n1, dtype=jnp.int32)
    ang = (2.0 * math.pi / n1) * ((i1[:, None] * i1[None, :]) % n1).astype(F32)
    c, s = jnp.cos(ang), jnp.sin(ang)
    outer = jnp.concatenate([jnp.concatenate([c, s], axis=1),
                             jnp.concatenate([-s, c], axis=1)], axis=0)
    k2 = jnp.arange(FFT_INNER, dtype=jnp.int32)
    kk = i1[:, None, None] + n1 * k2[None, :, None]
    ang = (2.0 * math.pi / seq) * ((kk * k2[None, None, :]) % seq).astype(F32)
    inner = jnp.concatenate([jnp.cos(ang), jnp.sin(ang)], axis=2)
    return chan, outer, inner


def _fourier(u, seq):
    t, width = u.shape
    n_seq = t // seq
    gw = width // FNET_GROUPS
    n1 = seq // FFT_INNER
    chan, outer, inner = _fnet_tables(seq, gw)
    tm = _pick(seq, 512, SUBLANES)
    per_seq = seq // tm
    v = pl.pallas_call(
        _fnet_chan_kernel,
        grid=(t // tm, FNET_GROUPS),
        in_specs=[pl.BlockSpec((tm, gw), lambda i, g: (i, g)),
                  pl.BlockSpec((gw, 2 * gw), lambda i, g: (0, 0))],
        out_specs=pl.BlockSpec((1, 2, tm, gw), lambda i, g: (i // per_seq, 0, i % per_seq, g)),
        out_shape=jax.ShapeDtypeStruct((n_seq, 2, seq, width), F32),
        compiler_params=_params("parallel", "parallel"),
        name="fnet_channel_dft",
    )(u, chan)
    cols = FFT_INNER * width
    tc = _pick(cols, 4096)
    a = pl.pallas_call(
        _fnet_outer_kernel,
        grid=(n_seq, cols // tc),
        in_specs=[pl.BlockSpec((1, 2, n1, tc), lambda b, j: (b, 0, 0, j)),
                  pl.BlockSpec((2 * n1, 2 * n1), lambda b, j: (0, 0))],
        out_specs=pl.BlockSpec((1, 2, n1, tc), lambda b, j: (b, 0, 0, j)),
        out_shape=jax.ShapeDtypeStruct((n_seq, 2, n1, cols), F32),
        compiler_params=_params("parallel", "parallel"),
        name="fnet_outer_dft",
    )(v.reshape(n_seq, 2, n1, cols), outer)
    f = pl.pallas_call(
        functools.partial(_fnet_inner_kernel, norm=1.0 / math.sqrt(seq * gw)),
        grid=(n_seq, n1),
        in_specs=[pl.BlockSpec((1, 2, 1, FFT_INNER, width), lambda b, k: (b, 0, k, 0, 0)),
                  pl.BlockSpec((1, FFT_INNER, 2 * FFT_INNER), lambda b, k: (k, 0, 0))],
        out_specs=pl.BlockSpec((1, FFT_INNER, width), lambda b, k: (b, 0, k)),
        out_shape=jax.ShapeDtypeStruct((n_seq, FFT_INNER, n1 * width), BF16),
        compiler_params=_params("parallel", "parallel"),
        name="fnet_inner_dft",
    )(a.reshape(n_seq, 2, n1, FFT_INNER, width), inner)
    return f.reshape(t, width)


def _rope_freqs(dim):
    return ROPE_THETA ** (-jnp.arange(0, dim, 2, dtype=F32) / dim)


def _pair_tables(ang_a, ang_b):
    ca, sa = jnp.cos(ang_a), jnp.sin(ang_a)
    zero = jnp.zeros_like(ca)
    if ang_b is None:
        cb, sb = zero, zero
    else:
        cb, sb = jnp.cos(ang_b), jnp.sin(ang_b)
    cos = jnp.concatenate([ca, ca, cb, cb], axis=1)
    sin_hi = jnp.concatenate([-sa, zero, -sb, zero], axis=1)
    sin_lo = jnp.concatenate([zero, sa, zero, sb], axis=1)
    return cos, sin_hi, sin_lo


def _pad_cols(w, width):
    return jnp.pad(w, ((0, 0), (0, width - w.shape[1])))


def _even_layer(x, mods, seq, norm_w, w_in, pool_w, pool_scale, q_norm, wq_b, kv_norm, wkv_b, w_out, tabs):
    d = x.shape[1]
    pool_width = pool_scale.shape[0]
    q_rank = q_norm.shape[0]
    kv_rank = kv_norm.shape[0]
    heads = wq_b.shape[1] // (MLA_NOPE + MLA_ROPE)
    in_width = -(-w_in.shape[1] // 256) * 256
    w_in_p = _pad_cols(w_in, in_width).astype(BF16)
    wq = wq_b.reshape(q_rank, heads, MLA_NOPE + MLA_ROPE)
    wq_pad = jnp.pad(wq, ((0, 0), (0, 0), (0, MLA_QK_PAD - MLA_NOPE - MLA_ROPE)))
    wq_pad = wq_pad.reshape(q_rank, heads * MLA_QK_PAD).astype(BF16)

    h = _normmod(x, norm_w, mods, seq, shift_row=0, scale_row=1)
    z = _matmul(h, w_in_p, F32, tm_target=1024, tn_target=768)
    a = _pool(z, pool_w.astype(BF16), pool_scale, seq)
    scale = (MLA_NOPE + MLA_ROPE) ** -0.5 * LOG2E
    q = _mla_q(z, pool_width // q_rank, q_norm, wq_pad, tabs, seq, scale)
    k, v = _mla_kv(z, (pool_width + q_rank) // kv_rank, (pool_width + q_rank + kv_rank) // LANES,
                   kv_norm, wkv_b.astype(BF16), tabs, seq)
    attn = _flash(q, k, v, seq, n_groups=heads // 2, n_heads=2, dq=MLA_QK_PAD, dv=MLA_V, shared_kv=False)
    w_out = w_out.astype(BF16)
    return _matmul_resid([a, attn], [w_out[:pool_width], w_out[pool_width:]], x, mods, seq,
                         gate_row=2, tm_target=1024, tn_target=512)


def _odd_layer(x, mods, seq, norm_w, w_in, q_norm, k_norm, fnet_w, w_out, tabs):
    fnet_width = fnet_w.shape[0]
    qkv_width = w_in.shape[1] - fnet_width
    kd = qkv_width // (GQA_GROUP + 2)
    qd = GQA_GROUP * kd
    w_in = w_in.astype(BF16)

    h = _normmod(x, norm_w, mods, seq, shift_row=0, scale_row=1)
    q = _gqa_qk(h, w_in[:, :qd], q_norm, tabs, seq, GQA_HEAD_DIM ** -0.5 * LOG2E)
    k = _gqa_qk(h, w_in[:, qd:qd + kd], k_norm, tabs, seq, 1.0)
    v = _matmul(h, w_in[:, qd + kd:qd + 2 * kd], BF16, tn_target=768)
    u = _matmul(h, w_in[:, qd + 2 * kd:], F32)
    attn = _flash(q, k, v, seq, n_groups=kd // GQA_HEAD_DIM, n_heads=GQA_GROUP,
                  dq=GQA_HEAD_DIM, dv=GQA_HEAD_DIM, shared_kv=True)
    f = _fourier(u, seq)
    ff = _matmul(f, fnet_w.astype(BF16), BF16)
    w_out = w_out.astype(BF16)
    return _matmul_resid([attn, ff], [w_out[:qd], w_out[qd:]], x, mods, seq,
                         gate_row=2, tm_target=1024, tn_target=512)


def _ffn(x, mods, seq, norm_w, w_gate, w_up, w_down):
    h = _normmod(x, norm_w, mods, seq, shift_row=3, scale_row=4)
    hid = _matmul_swiglu(h, w_gate.astype(BF16), w_up.astype(BF16))
    return _matmul_resid([hid], [w_down.astype(BF16)], x, mods, seq,
                         gate_row=5, tm_target=512, tn_target=256)


def kernel(x_prompt, x_sample, c_prompt, c_sample, mod_w, mod_b, norm_mix, norm_ffn, e_w_in, e_pool_w, e_pool_scale, e_q_norm, e_wq_b, e_kv_norm, e_wkv_b, e_w_out, o_w_in, o_q_norm, o_k_norm, o_fnet_w, o_w_out, ffn_gate, ffn_up, ffn_down, final_norm):
    n_prompt, seq, d = x_prompt.shape
    n_sample = x_sample.shape[0]
    assert x_sample.shape[1] == seq and seq % GRID_W == 0 and seq % FFT_INNER == 0
    n_seq = n_prompt + n_sample
    depth = mod_w.shape[0]

    x = jnp.concatenate([x_prompt.reshape(n_prompt * seq, d), x_sample.reshape(n_sample * seq, d)], axis=0)
    c = jnp.concatenate([c_prompt, c_sample], axis=0)
    c_pad = jnp.pad(c, ((0, -n_seq % SUBLANES), (0, 0)))
    mods_all = _modulation(c_pad, mod_w, mod_b).reshape(depth, c_pad.shape[0], 6, d)

    pos = jnp.arange(seq)
    ang_1d = pos.astype(F32)[:, None] * _rope_freqs(MLA_ROPE)[None, :]
    ang_row = (pos // GRID_W).astype(F32)[:, None] * _rope_freqs(GQA_HEAD_DIM // 2)[None, :]
    ang_col = (pos % GRID_W).astype(F32)[:, None] * _rope_freqs(GQA_HEAD_DIM // 2)[None, :]
    mla_tabs = _pair_tables(ang_1d, None)
    gqa_tabs = _pair_tables(ang_row, ang_col)

    for l in range(depth):
        mods = mods_all[l]
        i = l // 2
        if l % 2 == 0:
            x = _even_layer(x, mods, seq, norm_mix[l], e_w_in[i], e_pool_w[i], e_pool_scale[i], e_q_norm[i],
                            e_wq_b[i], e_kv_norm[i], e_wkv_b[i], e_w_out[i], mla_tabs)
        else:
            x = _odd_layer(x, mods, seq, norm_mix[l], o_w_in[i], o_q_norm[i], o_k_norm[i], o_fnet_w[i],
                           o_w_out[i], gqa_tabs)
        x = _ffn(x, mods, seq, norm_ffn[l], ffn_gate[l], ffn_up[l], ffn_down[l])

    y = _final_norm(x, final_norm)
    y_prompt = y[:n_prompt * seq].reshape(n_prompt, seq, d)
    y_sample = y[n_prompt * seq:].reshape(n_sample, seq, d)
    return (y_prompt, y_sample)
```

```python
import functools
import math

import jax
import jax.numpy as jnp
from jax import lax
from jax.experimental import pallas as pl
from jax.experimental.pallas import tpu as pltpu

F32 = jnp.float32
BF16 = jnp.bfloat16

EPS = 1e-6
ROPE_THETA = 10000.0
GRID_W = 64
POOL_WINDOWS = (2, 4, 8, 16)
MLA_NOPE = 128
MLA_ROPE = 64
MLA_V = 128
MLA_QK_PAD = 256
MLA_SHIFT_LANE = 32
GQA_HEAD_DIM = 128
GQA_GROUP = 4
FNET_GROUPS = 4
LOG2E = 1.4426950408889634

LANES = 128
SUBLANES = 8
VMEM_LIMIT_BYTES = 52 * 1024 * 1024

HIGHEST = lax.Precision.HIGHEST


def _pick(n, target, quantum=LANES):
    best = None
    t = quantum
    while t <= min(n, target):
        if n % t == 0:
            best = t
        t += quantum
    if best is None:
        return n
    return best


def _params(*sem):
    return pltpu.CompilerParams(dimension_semantics=sem, vmem_limit_bytes=VMEM_LIMIT_BYTES)


ROPE_SPLIT = LANES // 2


def _rot_pair(x, c, s):
    return x * c + pltpu.roll(x, ROPE_SPLIT, 1) * s


def _half_split_perm(width):
    q = width // 4
    idx = jnp.arange(width).reshape(2, 2, q)
    return idx.transpose(1, 0, 2).reshape(width)


def _mod_kernel(c_ref, w_ref, b_ref, o_ref):
    c = c_ref[...]
    act = (c / (1.0 + jnp.exp(-c))).astype(BF16)
    w = w_ref[0].astype(BF16)
    o_ref[0] = jnp.dot(act, w, preferred_element_type=F32) + b_ref[0]


def _modulation(c_pad, mod_w, mod_b):
    depth, d, n = mod_w.shape
    rows = c_pad.shape[0]
    tn = _pick(n, 512)
    return pl.pallas_call(
        _mod_kernel,
        grid=(depth, n // tn),
        in_specs=[
            pl.BlockSpec((rows, d), lambda l, j: (0, 0)),
            pl.BlockSpec((1, d, tn), lambda l, j: (l, 0, j)),
            pl.BlockSpec((1, 1, tn), lambda l, j: (l, 0, j)),
        ],
        out_specs=pl.BlockSpec((1, rows, tn), lambda l, j: (l, 0, j)),
        out_shape=jax.ShapeDtypeStruct((depth, rows, n), F32),
        compiler_params=_params("parallel", "parallel"),
        name="modulation",
    )(c_pad, mod_w, mod_b.reshape(depth, 1, n))


def _normmod_kernel(x_ref, w_ref, m_ref, o_ref, *, shift_row, scale_row):
    x = x_ref[...]
    y = x * lax.rsqrt(jnp.mean(x * x, axis=-1, keepdims=True) + EPS) * w_ref[...]
    sc = m_ref[0, scale_row:scale_row + 1, :]
    sh = m_ref[0, shift_row:shift_row + 1, :]
    o_ref[...] = (y * (1.0 + sc) + sh).astype(o_ref.dtype)


def _norm_kernel(x_ref, w_ref, o_ref):
    x = x_ref[...]
    y = x * lax.rsqrt(jnp.mean(x * x, axis=-1, keepdims=True) + EPS) * w_ref[...]
    o_ref[...] = y.astype(o_ref.dtype)


def _normmod(x, w, mods, seq, shift_row, scale_row):
    t, d = x.shape
    tm = _pick(seq, 256, SUBLANES)
    per_seq = seq // tm
    return pl.pallas_call(
        functools.partial(_normmod_kernel, shift_row=shift_row, scale_row=scale_row),
        grid=(t // tm,),
        in_specs=[
            pl.BlockSpec((tm, d), lambda i: (i, 0)),
            pl.BlockSpec((1, d), lambda i: (0, 0)),
            pl.BlockSpec((1, 6, d), lambda i: (i // per_seq, 0, 0)),
        ],
        out_specs=pl.BlockSpec((tm, d), lambda i: (i, 0)),
        out_shape=jax.ShapeDtypeStruct((t, d), BF16),
        compiler_params=_params("parallel"),
        name="normmod",
    )(x, w.reshape(1, d), mods)


def _final_norm(x, w):
    t, d = x.shape
    tm = _pick(t, 256, SUBLANES)
    return pl.pallas_call(
        _norm_kernel,
        grid=(t // tm,),
        in_specs=[pl.BlockSpec((tm, d), lambda i: (i, 0)),
                  pl.BlockSpec((1, d), lambda i: (0, 0))],
        out_specs=pl.BlockSpec((tm, d), lambda i: (i, 0)),
        out_shape=jax.ShapeDtypeStruct((t, d), F32),
        compiler_params=_params("parallel"),
        name="final_norm",
    )(x, w.reshape(1, d))


def _mm_plain_kernel(a_ref, w_ref, o_ref):
    o_ref[...] = jnp.dot(a_ref[...], w_ref[...], preferred_element_type=F32).astype(o_ref.dtype)


def _matmul(a, w, out_dtype, tm_target=1024, tn_target=512):
    m, k = a.shape
    n = w.shape[1]
    tm = _pick(m, tm_target, SUBLANES)
    tn = _pick(n, tn_target)
    return pl.pallas_call(
        _mm_plain_kernel,
        grid=(m // tm, n // tn),
        in_specs=[pl.BlockSpec((tm, k), lambda i, j: (i, 0)),
                  pl.BlockSpec((k, tn), lambda i, j: (0, j))],
        out_specs=pl.BlockSpec((tm, tn), lambda i, j: (i, j)),
        out_shape=jax.ShapeDtypeStruct((m, n), out_dtype),
        compiler_params=_params("parallel", "parallel"),
        name="matmul",
    )(a, w)


def _mm_resid_kernel(*refs, n_a, gate_row):
    a_refs = refs[:n_a]
    w_refs = refs[n_a:2 * n_a]
    x_ref, m_ref, o_ref = refs[2 * n_a:]
    acc = jnp.dot(a_refs[0][...], w_refs[0][...], preferred_element_type=F32)
    for a_ref, w_ref in zip(a_refs[1:], w_refs[1:]):
        acc = acc + jnp.dot(a_ref[...], w_ref[...], preferred_element_type=F32)
    o_ref[...] = x_ref[...] + m_ref[0, gate_row:gate_row + 1, :] * acc


def _matmul_resid(a_list, w_list, x, mods, seq, gate_row, tm_target, tn_target):
    t, n = x.shape
    tm = _pick(seq, tm_target, SUBLANES)
    tn = _pick(n, tn_target)
    per_seq = seq // tm
    n_a = len(a_list)
    in_specs = [pl.BlockSpec((tm, a.shape[1]), lambda i, j: (i, 0)) for a in a_list]
    in_specs += [pl.BlockSpec((w.shape[0], tn), lambda i, j: (0, j)) for w in w_list]
    in_specs += [pl.BlockSpec((tm, tn), lambda i, j: (i, j)),
                 pl.BlockSpec((1, 6, tn), lambda i, j: (i // per_seq, 0, j))]
    return pl.pallas_call(
        functools.partial(_mm_resid_kernel, n_a=n_a, gate_row=gate_row),
        grid=(t // tm, n // tn),
        in_specs=in_specs,
        out_specs=pl.BlockSpec((tm, tn), lambda i, j: (i, j)),
        out_shape=jax.ShapeDtypeStruct((t, n), F32),
        compiler_params=_params("parallel", "parallel"),
        name="matmul_resid",
    )(*a_list, *w_list, x, mods)


def _mm_swiglu_kernel(a_ref, wg_ref, wu_ref, o_ref):
    a = a_ref[...]
    g = jnp.dot(a, wg_ref[...], preferred_element_type=F32)
    u = jnp.dot(a, wu_ref[...], preferred_element_type=F32)
    o_ref[...] = (g / (1.0 + jnp.exp(-g)) * u).astype(o_ref.dtype)


def _matmul_swiglu(a, wg, wu, tm_target=2048, tn_target=256):
    m, k = a.shape
    n = wg.shape[1]
    tm = _pick(m, tm_target, SUBLANES)
    tn = _pick(n, tn_target)
    return pl.pallas_call(
        _mm_swiglu_kernel,
        grid=(m // tm, n // tn),
        in_specs=[pl.BlockSpec((tm, k), lambda i, j: (i, 0)),
                  pl.BlockSpec((k, tn), lambda i, j: (0, j)),
                  pl.BlockSpec((k, tn), lambda i, j: (0, j))],
        out_specs=pl.BlockSpec((tm, tn), lambda i, j: (i, j)),
        out_shape=jax.ShapeDtypeStruct((m, n), BF16),
        compiler_params=_params("parallel", "parallel"),
        name="matmul_swiglu",
    )(a, wg, wu)


def _mla_q_kernel(a_ref, nw_ref, w_ref, c_ref, s_ref, o_ref, *, scale):
    a = a_ref[...]
    nrm = (a * lax.rsqrt(jnp.mean(a * a, axis=-1, keepdims=True) + EPS) * nw_ref[...]).astype(BF16)
    c, s = c_ref[...], s_ref[...]
    for h in range(o_ref.shape[1] // MLA_QK_PAD):
        lo = h * MLA_QK_PAD
        acc = jnp.dot(nrm, w_ref[:, lo:lo + MLA_QK_PAD], preferred_element_type=F32)
        o_ref[:, lo:lo + MLA_NOPE] = (acc[:, :MLA_NOPE] * scale).astype(o_ref.dtype)
        o_ref[:, lo + MLA_NOPE:lo + MLA_QK_PAD] = (_rot_pair(acc[:, MLA_NOPE:], c, s) * scale).astype(o_ref.dtype)


def _mla_q(z, col_block, q_norm, wq_pad, tabs, seq, scale):
    t = z.shape[0]
    rank, n = wq_pad.shape
    tm = _pick(seq, 512, SUBLANES)
    per_seq = seq // tm
    tab_spec = pl.BlockSpec((tm, LANES), lambda i: (i % per_seq, 0))
    return pl.pallas_call(
        functools.partial(_mla_q_kernel, scale=scale),
        grid=(t // tm,),
        in_specs=[pl.BlockSpec((tm, rank), lambda i: (i, col_block)),
                  pl.BlockSpec((1, rank), lambda i: (0, 0)),
                  pl.BlockSpec((rank, n), lambda i: (0, 0)),
                  tab_spec, tab_spec],
        out_specs=pl.BlockSpec((tm, n), lambda i: (i, 0)),
        out_shape=jax.ShapeDtypeStruct((t, n), BF16),
        compiler_params=_params("parallel"),
        name="mla_q_proj",
    )(z, q_norm.reshape(1, rank), wq_pad, *tabs)


def _mla_kv_kernel(a_ref, kr_ref, nw_ref, w_ref, c_ref, s_ref, k_ref, v_ref):
    a = a_ref[...]
    nrm = (a * lax.rsqrt(jnp.mean(a * a, axis=-1, keepdims=True) + EPS) * nw_ref[...]).astype(BF16)
    kr = _rot_pair(kr_ref[...], c_ref[...], s_ref[...])
    lane = lax.broadcasted_iota(jnp.int32, kr.shape, 1)
    kr = jnp.where(lane == MLA_SHIFT_LANE, 1.0, kr).astype(k_ref.dtype)
    ones = _ones_col(kr.shape[0], v_ref.dtype)
    width = MLA_NOPE + MLA_V
    for h in range(w_ref.shape[1] // width):
        acc = jnp.dot(nrm, w_ref[:, h * width:(h + 1) * width], preferred_element_type=F32)
        k_ref[:, h * MLA_QK_PAD:h * MLA_QK_PAD + MLA_NOPE] = acc[:, :MLA_NOPE].astype(k_ref.dtype)
        k_ref[:, h * MLA_QK_PAD + MLA_NOPE:(h + 1) * MLA_QK_PAD] = kr
        v_ref[:, h * ATTN_PAD:h * ATTN_PAD + MLA_V] = acc[:, MLA_NOPE:].astype(v_ref.dtype)
        v_ref[:, h * ATTN_PAD + MLA_V:(h + 1) * ATTN_PAD] = ones


def _mla_kv(z, kv_col_block, kr_col_block, kv_norm, wkv, tabs, seq):
    t = z.shape[0]
    rank, n = wkv.shape
    heads = n // (MLA_NOPE + MLA_V)
    tm = _pick(seq, 512, SUBLANES)
    per_seq = seq // tm
    tab_spec = pl.BlockSpec((tm, LANES), lambda i: (i % per_seq, 0))
    return pl.pallas_call(
        _mla_kv_kernel,
        grid=(t // tm,),
        in_specs=[pl.BlockSpec((tm, rank), lambda i: (i, kv_col_block)),
                  pl.BlockSpec((tm, LANES), lambda i: (i, kr_col_block)),
                  pl.BlockSpec((1, rank), lambda i: (0, 0)),
                  pl.BlockSpec((rank, n), lambda i: (0, 0)),
                  tab_spec, tab_spec],
        out_specs=[pl.BlockSpec((tm, heads * MLA_QK_PAD), lambda i: (i, 0)),
                   pl.BlockSpec((tm, heads * ATTN_PAD), lambda i: (i, 0))],
        out_shape=[jax.ShapeDtypeStruct((t, heads * MLA_QK_PAD), BF16),
                   jax.ShapeDtypeStruct((t, heads * ATTN_PAD), BF16)],
        compiler_params=_params("parallel"),
        name="mla_kv_proj",
    )(z, z, kv_norm.reshape(1, rank), wkv, *tabs)


MXU_WIDTH = 256


def _head_norm_rope(x, nw, c, s):
    y = x * lax.rsqrt(jnp.mean(x * x, axis=-1, keepdims=True) + EPS) * nw
    return _rot_pair(y, c, s)


def _gqa_qk_kernel(a_ref, w_ref, nw_ref, c_ref, s_ref, o_ref, *, scale):
    a = a_ref[...]
    nw, c, s = nw_ref[...], c_ref[...], s_ref[...]
    for p in range(o_ref.shape[1] // MXU_WIDTH):
        acc = jnp.dot(a, w_ref[:, p * MXU_WIDTH:(p + 1) * MXU_WIDTH], preferred_element_type=F32)
        for h in range(MXU_WIDTH // GQA_HEAD_DIM):
            lo = h * GQA_HEAD_DIM
            y = _head_norm_rope(acc[:, lo:lo + GQA_HEAD_DIM], nw, c, s) * scale
            o_ref[:, p * MXU_WIDTH + lo:p * MXU_WIDTH + lo + GQA_HEAD_DIM] = y.astype(o_ref.dtype)


def _gqa_qk(a, w, head_norm, tabs, seq, scale):
    t, k = a.shape
    n = w.shape[1]
    tm = _pick(seq, 1024, SUBLANES)
    tn = _pick(n, 512)
    per_seq = seq // tm
    tab_spec = pl.BlockSpec((tm, LANES), lambda i, j: (i % per_seq, 0))
    return pl.pallas_call(
        functools.partial(_gqa_qk_kernel, scale=scale),
        grid=(t // tm, n // tn),
        in_specs=[pl.BlockSpec((tm, k), lambda i, j: (i, 0)),
                  pl.BlockSpec((k, tn), lambda i, j: (0, j)),
                  pl.BlockSpec((1, GQA_HEAD_DIM), lambda i, j: (0, 0)),
                  tab_spec, tab_spec],
        out_specs=pl.BlockSpec((tm, tn), lambda i, j: (i, j)),
        out_shape=jax.ShapeDtypeStruct((t, n), BF16),
        compiler_params=_params("parallel", "parallel"),
        name="gqa_qk_proj",
    )(a, w, head_norm.reshape(1, GQA_HEAD_DIM), *tabs)


def _gqa_kv_kernel(a_ref, w_ref, nw_ref, c_ref, s_ref, k_ref, v_ref):
    a = a_ref[...]
    nw, c, s = nw_ref[...], c_ref[...], s_ref[...]
    kd = w_ref.shape[1] // 2
    ones = _ones_col(a.shape[0], k_ref.dtype)
    per_dot = MXU_WIDTH // GQA_HEAD_DIM
    for p in range(w_ref.shape[1] // MXU_WIDTH):
        acc = jnp.dot(a, w_ref[:, p * MXU_WIDTH:(p + 1) * MXU_WIDTH], preferred_element_type=F32)
        for h in range(per_dot):
            x = acc[:, h * GQA_HEAD_DIM:(h + 1) * GQA_HEAD_DIM]
            col = p * MXU_WIDTH + h * GQA_HEAD_DIM
            if col < kd:
                head = col // GQA_HEAD_DIM
                k_ref[:, head * ATTN_PAD:head * ATTN_PAD + GQA_HEAD_DIM] = (
                    _head_norm_rope(x, nw, c, s).astype(k_ref.dtype))
                k_ref[:, head * ATTN_PAD + GQA_HEAD_DIM:(head + 1) * ATTN_PAD] = ones
            else:
                head = (col - kd) // GQA_HEAD_DIM
                v_ref[:, head * ATTN_PAD:head * ATTN_PAD + GQA_HEAD_DIM] = x.astype(v_ref.dtype)
                v_ref[:, head * ATTN_PAD + GQA_HEAD_DIM:(head + 1) * ATTN_PAD] = ones


def _gqa_kv(a, w_kv, k_norm, tabs, seq):
    t, k = a.shape
    n = w_kv.shape[1]
    heads = n // (2 * GQA_HEAD_DIM)
    tm = _pick(seq, 512, SUBLANES)
    per_seq = seq // tm
    tab_spec = pl.BlockSpec((tm, LANES), lambda i: (i % per_seq, 0))
    out_spec = pl.BlockSpec((tm, heads * ATTN_PAD), lambda i: (i, 0))
    out_shape = jax.ShapeDtypeStruct((t, heads * ATTN_PAD), BF16)
    return pl.pallas_call(
        _gqa_kv_kernel,
        grid=(t // tm,),
        in_specs=[pl.BlockSpec((tm, k), lambda i: (i, 0)),
                  pl.BlockSpec((k, n), lambda i: (0, 0)),
                  pl.BlockSpec((1, GQA_HEAD_DIM), lambda i: (0, 0)),
                  tab_spec, tab_spec],
        out_specs=[out_spec, out_spec],
        out_shape=[out_shape, out_shape],
        compiler_params=_params("parallel"),
        name="gqa_kv_proj",
    )(a, w_kv, k_norm.reshape(1, GQA_HEAD_DIM), *tabs)


ATTN_PAD = 256
FAST_PATH_MIN_SUM = 2.0 ** -60
BOUND_MARGIN = 1.0 + 2.0 ** -6
NT_DIMS = (((1,), (1,)), ((), ()))


def _ones_col(rows, dtype):
    lane = lax.broadcasted_iota(jnp.int32, (rows, LANES), 1)
    return jnp.where(lane == 0, 1.0, 0.0).astype(dtype)


def _flash_kernel(q_ref, k_ref, v_ref, o_ref, kmax_scr, acc_scr, m_scr, l_scr, *,
                  n_heads, dq, dv, shared_kv, shift_col, chunk):
    seq = k_ref.shape[0]
    tq = q_ref.shape[0]
    n_chunks = seq // chunk
    n_kv = 1 if shared_kv else n_heads

    def kv_chunk(c, hk):
        r0 = pl.multiple_of(c * chunk, chunk)
        cols = slice(hk * ATTN_PAD, (hk + 1) * ATTN_PAD)
        return k_ref[pl.ds(r0, chunk), cols], v_ref[pl.ds(r0, chunk), cols]

    @pl.when(pl.program_id(2) == 0)
    def _():
        for hk in range(n_kv):
            def norm_body(c, best):
                kf = kv_chunk(c, hk)[0].astype(F32)
                row = jnp.sum(kf * kf, axis=-1, keepdims=True)
                return jnp.maximum(best, jnp.max(row, axis=0, keepdims=True))
            best = lax.fori_loop(0, n_chunks, norm_body, jnp.zeros((1, 1), F32))
            kmax_scr[hk] = jnp.broadcast_to(jnp.sqrt(best), kmax_scr.shape[1:])

    lane = lax.broadcasted_iota(jnp.int32, (tq, LANES), 1)
    ok = None
    for h in range(n_heads):
        hk = 0 if shared_kv else h
        q = q_ref[:, h * dq:(h + 1) * dq]
        qf = q.astype(F32)
        shift = jnp.sqrt(jnp.sum(qf * qf, axis=-1, keepdims=True)) * kmax_scr[hk][0:1, 0:1] * BOUND_MARGIN
        hi = qf[:, LANES:] if dq > LANES else jnp.zeros((tq, LANES), F32)
        hi = jnp.where(lane == shift_col - LANES, -shift, hi).astype(q.dtype)
        qx = jnp.concatenate([q[:, :LANES], hi], axis=1)
        acc_scr[...] = jnp.zeros(acc_scr.shape, F32)

        def fast_body(c, carry):
            kc, vc = kv_chunk(c, hk)
            s = lax.dot_general(qx, kc, NT_DIMS, preferred_element_type=F32)
            acc_scr[...] += jnp.dot(jnp.exp2(s).astype(vc.dtype), vc, preferred_element_type=F32)
            return carry

        lax.fori_loop(0, n_chunks, fast_body, 0, unroll=4)
        acc = acc_scr[...]
        l = acc[:, dv:dv + 1]
        o_ref[:, h * dv:(h + 1) * dv] = (acc[:, :dv] / l).astype(o_ref.dtype)
        head_ok = jnp.min(l) >= FAST_PATH_MIN_SUM
        ok = head_ok if ok is None else jnp.logical_and(ok, head_ok)

    @pl.when(jnp.logical_not(ok))
    def _():
        for h in range(n_heads):
            hk = 0 if shared_kv else h
            q = q_ref[:, h * dq:(h + 1) * dq]
            m_scr[...] = jnp.full(m_scr.shape, -jnp.inf, F32)
            l_scr[...] = jnp.zeros(l_scr.shape, F32)
            acc_scr[...] = jnp.zeros(acc_scr.shape, F32)

            def exact_body(c, carry):
                kc, vc = kv_chunk(c, hk)
                s = lax.dot_general(q, kc[:, :dq], NT_DIMS, preferred_element_type=F32)
                m_prev = m_scr[:, 0:1]
                m_new = jnp.maximum(m_prev, jnp.max(s, axis=-1, keepdims=True))
                alpha = jnp.exp2(m_prev - m_new)
                p = jnp.exp2(s - m_new)
                l_new = alpha * l_scr[:, 0:1] + jnp.sum(p, axis=-1, keepdims=True)
                acc_scr[...] = alpha * acc_scr[...] + jnp.dot(p.astype(vc.dtype), vc, preferred_element_type=F32)
                m_scr[...] = jnp.broadcast_to(m_new, m_scr.shape)
                l_scr[...] = jnp.broadcast_to(l_new, l_scr.shape)
                return carry

            lax.fori_loop(0, n_chunks, exact_body, 0)
            o_ref[:, h * dv:(h + 1) * dv] = (acc_scr[:, :dv] / l_scr[:, 0:1]).astype(o_ref.dtype)


def _flash(q, k, v, seq, n_groups, n_heads, dq, shared_kv, shift_col, tq_target=512, chunk_target=1024):
    t = q.shape[0]
    n_seq = t // seq
    dv = ATTN_PAD // 2
    tq = _pick(seq, tq_target, SUBLANES)
    chunk = _pick(seq, chunk_target, SUBLANES)
    qb = seq // tq
    n_kv = 1 if shared_kv else n_heads
    return pl.pallas_call(
        functools.partial(_flash_kernel, n_heads=n_heads, dq=dq, dv=dv, shared_kv=shared_kv,
                          shift_col=shift_col, chunk=chunk),
        grid=(n_seq, n_groups, qb),
        in_specs=[pl.BlockSpec((tq, n_heads * dq), lambda b, g, i: (b * qb + i, g)),
                  pl.BlockSpec((seq, n_kv * ATTN_PAD), lambda b, g, i: (b, g)),
                  pl.BlockSpec((seq, n_kv * ATTN_PAD), lambda b, g, i: (b, g))],
        out_specs=pl.BlockSpec((tq, n_heads * dv), lambda b, g, i: (b * qb + i, g)),
        out_shape=jax.ShapeDtypeStruct((t, n_groups * n_heads * dv), BF16),
        scratch_shapes=[pltpu.VMEM((n_kv, SUBLANES, LANES), F32),
                        pltpu.VMEM((tq, ATTN_PAD), F32),
                        pltpu.VMEM((tq, LANES), F32),
                        pltpu.VMEM((tq, LANES), F32)],
        compiler_params=_params("parallel", "parallel", "arbitrary"),
        name="flash_attention",
    )(q, k, v)


POOL_HALO = 16


def _pool_kernel(prev_ref, cur_ref, next_ref, w_ref, s_ref, o_ref, ext_ref, *, seq, group):
    tm = cur_ref.shape[0]
    ext_ref[0:POOL_HALO, :] = prev_ref[...]
    ext_ref[POOL_HALO:POOL_HALO + tm, :] = cur_ref[...]
    ext_ref[POOL_HALO + tm:, :] = next_ref[...]
    pos = (pl.program_id(0) * tm) % seq + lax.broadcasted_iota(jnp.int32, (tm, 1), 0)
    for gi, win in enumerate(POOL_WINDOWS):
        cols = slice(gi * group, (gi + 1) * group)
        total = jnp.zeros((tm, group), F32)
        count = jnp.zeros((tm, 1), F32)
        for off in range(-(win // 2), win - win // 2):
            valid = jnp.logical_and(pos + off >= 0, pos + off < seq)
            term = ext_ref[POOL_HALO + off:POOL_HALO + off + tm, cols]
            total = total + jnp.where(valid, term, 0.0)
            count = count + valid.astype(F32)
        diff = (total / count - cur_ref[:, cols]).astype(BF16)
        y = jnp.dot(diff, w_ref[gi], preferred_element_type=F32) * s_ref[:, cols]
        o_ref[:, cols] = y.astype(o_ref.dtype)


def _pool(z, pool_w, pool_scale, seq):
    t = z.shape[0]
    n_groups, group, _ = pool_w.shape
    width = n_groups * group
    tm = _pick(seq, 256, POOL_HALO)
    halo_blocks = tm // POOL_HALO
    last_halo = t // POOL_HALO - 1
    return pl.pallas_call(
        functools.partial(_pool_kernel, seq=seq, group=group),
        grid=(t // tm,),
        in_specs=[
            pl.BlockSpec((POOL_HALO, width), lambda i: (jnp.maximum(i * halo_blocks - 1, 0), 0)),
            pl.BlockSpec((tm, width), lambda i: (i, 0)),
            pl.BlockSpec((POOL_HALO, width), lambda i: (jnp.minimum((i + 1) * halo_blocks, last_halo), 0)),
            pl.BlockSpec((n_groups, group, group), lambda i: (0, 0, 0)),
            pl.BlockSpec((1, width), lambda i: (0, 0)),
        ],
        out_specs=pl.BlockSpec((tm, width), lambda i: (i, 0)),
        out_shape=jax.ShapeDtypeStruct((t, width), BF16),
        scratch_shapes=[pltpu.VMEM((tm + 2 * POOL_HALO, width), F32)],
        compiler_params=_params("parallel"),
        name="pool_mixer",
    )(z, z, z, pool_w, pool_scale.reshape(1, width))


FFT_INNER = 128


def _fnet_chan_kernel(u_ref, m_ref, o_ref):
    gw = u_ref.shape[1]
    res = jnp.dot(u_ref[...], m_ref[...], precision=HIGHEST, preferred_element_type=F32)
    o_ref[0, 0] = res[:, :gw]
    o_ref[0, 1] = res[:, gw:]


def _fnet_outer_kernel(v_ref, m_ref, o_ref):
    two, n1, tc = v_ref.shape[1:]
    x = v_ref[0].reshape(two * n1, tc)
    y = jnp.dot(m_ref[...], x, precision=HIGHEST, preferred_element_type=F32)
    o_ref[0] = y.reshape(two, n1, tc)


def _fnet_inner_kernel(a_ref, g_ref, o_ref, *, norm):
    x = a_ref[0, :, 0]
    x = x.reshape(x.shape[0] * x.shape[1], x.shape[2])
    y = jnp.dot(g_ref[0], x, precision=HIGHEST, preferred_element_type=F32)
    o_ref[0] = (y * norm).astype(o_ref.dtype)


def _fnet_tables(seq, gw):
    n1 = seq // FFT_INNER
    ch = jnp.arange(gw, dtype=jnp.int32)
    ang = (2.0 * math.pi / gw) * ((ch[:, None] * ch[None, :]) % gw).astype(F32)
    chan = jnp.concatenate([jnp.cos(ang), -jnp.sin(ang)], axis=1)
    i1 = jnp.arange(n1, dtype=jnp.int32)
    ang = (2.0 * math.pi / n1) * ((i1[:, None] * i1[None, :]) % n1).astype(F32)
    c, s = jnp.cos(ang), jnp.sin(ang)
    outer = jnp.concatenate([jnp.concatenate([c, s], axis=1),
                             jnp.concatenate([-s, c], axis=1)], axis=0)
    k2 = jnp.arange(FFT_INNER, dtype=jnp.int32)
    kk = i1[:, None, None] + n1 * k2[None, :, None]
    ang = (2.0 * math.pi / seq) * ((kk * k2[None, None, :]) % seq).astype(F32)
    inner = jnp.concatenate([jnp.cos(ang), jnp.sin(ang)], axis=2)
    return chan, outer, inner


def _fourier(u, seq):
    t, width = u.shape
    n_seq = t // seq
    gw = width // FNET_GROUPS
    n1 = seq // FFT_INNER
    chan, outer, inner = _fnet_tables(seq, gw)
    tm = _pick(seq, 512, SUBLANES)
    per_seq = seq // tm
    v = pl.pallas_call(
        _fnet_chan_kernel,
        grid=(t // tm, FNET_GROUPS),
        in_specs=[pl.BlockSpec((tm, gw), lambda i, g: (i, g)),
                  pl.BlockSpec((gw, 2 * gw), lambda i, g: (0, 0))],
        out_specs=pl.BlockSpec((1, 2, tm, gw), lambda i, g: (i // per_seq, 0, i % per_seq, g)),
        out_shape=jax.ShapeDtypeStruct((n_seq, 2, seq, width), F32),
        compiler_params=_params("parallel", "parallel"),
        name="fnet_channel_dft",
    )(u, chan)
    cols = FFT_INNER * width
    tc = _pick(cols, 4096)
    a = pl.pallas_call(
        _fnet_outer_kernel,
        grid=(n_seq, cols // tc),
        in_specs=[pl.BlockSpec((1, 2, n1, tc), lambda b, j: (b, 0, 0, j)),
                  pl.BlockSpec((2 * n1, 2 * n1), lambda b, j: (0, 0))],
        out_specs=pl.BlockSpec((1, 2, n1, tc), lambda b, j: (b, 0, 0, j)),
        out_shape=jax.ShapeDtypeStruct((n_seq, 2, n1, cols), F32),
        compiler_params=_params("parallel", "parallel"),
        name="fnet_outer_dft",
    )(v.reshape(n_seq, 2, n1, cols), outer)
    f = pl.pallas_call(
        functools.partial(_fnet_inner_kernel, norm=1.0 / math.sqrt(seq * gw)),
        grid=(n_seq, n1),
        in_specs=[pl.BlockSpec((1, 2, 1, FFT_INNER, width), lambda b, k: (b, 0, k, 0, 0)),
                  pl.BlockSpec((1, FFT_INNER, 2 * FFT_INNER), lambda b, k: (k, 0, 0))],
        out_specs=pl.BlockSpec((1, FFT_INNER, width), lambda b, k: (b, 0, k)),
        out_shape=jax.ShapeDtypeStruct((n_seq, FFT_INNER, n1 * width), BF16),
        compiler_params=_params("parallel", "parallel"),
        name="fnet_inner_dft",
    )(a.reshape(n_seq, 2, n1, FFT_INNER, width), inner)
    return f.reshape(t, width)


def _rope_freqs(dim):
    return ROPE_THETA ** (-jnp.arange(0, dim, 2, dtype=F32) / dim)


def _pair_tables(ang_a, ang_b):
    ca, sa = jnp.cos(ang_a), jnp.sin(ang_a)
    zero = jnp.zeros_like(ca)
    if ang_b is None:
        cb, sb = zero, zero
    else:
        cb, sb = jnp.cos(ang_b), jnp.sin(ang_b)
    cos = jnp.concatenate([ca, cb, ca, cb], axis=1)
    sin = jnp.concatenate([-sa, -sb, sa, sb], axis=1)
    return cos, sin


def _pad_cols(w, width):
    return jnp.pad(w, ((0, 0), (0, width - w.shape[1])))


def _even_layer(x, mods, seq, norm_w, w_in, pool_w, pool_scale, q_norm, wq_b, kv_norm, wkv_b, w_out, tabs):
    d = x.shape[1]
    pool_width = pool_scale.shape[0]
    q_rank = q_norm.shape[0]
    kv_rank = kv_norm.shape[0]
    heads = wq_b.shape[1] // (MLA_NOPE + MLA_ROPE)
    half = MLA_ROPE // 2
    kr_lo = pool_width + q_rank + kv_rank
    w_in_p = jnp.concatenate([w_in[:, :kr_lo], _pad_cols(w_in[:, kr_lo:kr_lo + half], ROPE_SPLIT),
                              _pad_cols(w_in[:, kr_lo + half:], ROPE_SPLIT)], axis=1)
    w_in_p = _pad_cols(w_in_p, -(-w_in_p.shape[1] // MXU_WIDTH) * MXU_WIDTH).astype(BF16)
    wq = wq_b.reshape(q_rank, heads, MLA_NOPE + MLA_ROPE)
    spare = jnp.zeros((q_rank, heads, ROPE_SPLIT - half), wq.dtype)
    wq_pad = jnp.concatenate([wq[..., :MLA_NOPE], wq[..., MLA_NOPE:MLA_NOPE + half], spare,
                              wq[..., MLA_NOPE + half:], spare], axis=-1)
    wq_pad = wq_pad.reshape(q_rank, heads * MLA_QK_PAD).astype(BF16)

    h = _normmod(x, norm_w, mods, seq, shift_row=0, scale_row=1)
    z = _matmul(h, w_in_p, F32, tm_target=1024, tn_target=768)
    a = _pool(z, pool_w.astype(BF16), pool_scale, seq)
    scale = (MLA_NOPE + MLA_ROPE) ** -0.5 * LOG2E
    q = _mla_q(z, pool_width // q_rank, q_norm, wq_pad, tabs, seq, scale)
    k, v = _mla_kv(z, (pool_width + q_rank) // kv_rank, (pool_width + q_rank + kv_rank) // LANES,
                   kv_norm, wkv_b.astype(BF16), tabs, seq)
    attn = _flash(q, k, v, seq, n_groups=heads // 2, n_heads=2, dq=MLA_QK_PAD, shared_kv=False,
                  shift_col=MLA_NOPE + MLA_SHIFT_LANE)
    w_out = w_out.astype(BF16)
    return _matmul_resid([a, attn], [w_out[:pool_width], w_out[pool_width:]], x, mods, seq,
                         gate_row=2, tm_target=1024, tn_target=512)


def _odd_layer(x, mods, seq, norm_w, w_in, q_norm, k_norm, fnet_w, w_out, tabs):
    fnet_width = fnet_w.shape[0]
    qkv_width = w_in.shape[1] - fnet_width
    kd = qkv_width // (GQA_GROUP + 2)
    qd = GQA_GROUP * kd
    w_in = w_in.astype(BF16)
    perm = _half_split_perm(GQA_HEAD_DIM)
    d_in = w_in.shape[0]
    w_q = w_in[:, :qd].reshape(d_in, qd // GQA_HEAD_DIM, GQA_HEAD_DIM)[:, :, perm].reshape(d_in, qd)
    w_k = w_in[:, qd:qd + kd].reshape(d_in, kd // GQA_HEAD_DIM, GQA_HEAD_DIM)[:, :, perm].reshape(d_in, kd)
    w_kv = jnp.concatenate([w_k, w_in[:, qd + kd:qd + 2 * kd]], axis=1)

    h = _normmod(x, norm_w, mods, seq, shift_row=0, scale_row=1)
    q = _gqa_qk(h, w_q, q_norm[perm], tabs, seq, GQA_HEAD_DIM ** -0.5 * LOG2E)
    k, v = _gqa_kv(h, w_kv, k_norm[perm], tabs, seq)
    u = _matmul(h, w_in[:, qd + 2 * kd:], F32)
    attn = _flash(q, k, v, seq, n_groups=kd // GQA_HEAD_DIM, n_heads=GQA_GROUP,
                  dq=GQA_HEAD_DIM, shared_kv=True, shift_col=GQA_HEAD_DIM)
    f = _fourier(u, seq)
    ff = _matmul(f, fnet_w.astype(BF16), BF16)
    w_out = w_out.astype(BF16)
    return _matmul_resid([attn, ff], [w_out[:qd], w_out[qd:]], x, mods, seq,
                         gate_row=2, tm_target=1024, tn_target=512)


def _ffn(x, mods, seq, norm_w, w_gate, w_up, w_down):
    h = _normmod(x, norm_w, mods, seq, shift_row=3, scale_row=4)
    hid = _matmul_swiglu(h, w_gate.astype(BF16), w_up.astype(BF16))
    return _matmul_resid([hid], [w_down.astype(BF16)], x, mods, seq,
                         gate_row=5, tm_target=512, tn_target=256)


def kernel(x_prompt, x_sample, c_prompt, c_sample, mod_w, mod_b, norm_mix, norm_ffn, e_w_in, e_pool_w, e_pool_scale, e_q_norm, e_wq_b, e_kv_norm, e_wkv_b, e_w_out, o_w_in, o_q_norm, o_k_norm, o_fnet_w, o_w_out, ffn_gate, ffn_up, ffn_down, final_norm):
    n_prompt, seq, d = x_prompt.shape
    n_sample = x_sample.shape[0]
    assert x_sample.shape[1] == seq and seq % GRID_W == 0 and seq % FFT_INNER == 0
    n_seq = n_prompt + n_sample
    depth = mod_w.shape[0]

    x = jnp.concatenate([x_prompt.reshape(n_prompt * seq, d), x_sample.reshape(n_sample * seq, d)], axis=0)
    c = jnp.concatenate([c_prompt, c_sample], axis=0)
    c_pad = jnp.pad(c, ((0, -n_seq % SUBLANES), (0, 0)))
    mods_all = _modulation(c_pad, mod_w, mod_b).reshape(depth, c_pad.shape[0], 6, d)

    pos = jnp.arange(seq)
    ang_1d = pos.astype(F32)[:, None] * _rope_freqs(MLA_ROPE)[None, :]
    ang_row = (pos // GRID_W).astype(F32)[:, None] * _rope_freqs(GQA_HEAD_DIM // 2)[None, :]
    ang_col = (pos % GRID_W).astype(F32)[:, None] * _rope_freqs(GQA_HEAD_DIM // 2)[None, :]
    mla_tabs = _pair_tables(ang_1d, None)
    gqa_tabs = _pair_tables(ang_row, ang_col)

    for l in range(depth):
        mods = mods_all[l]
        i = l // 2
        if l % 2 == 0:
            x = _even_layer(x, mods, seq, norm_mix[l], e_w_in[i], e_pool_w[i], e_pool_scale[i], e_q_norm[i],
                            e_wq_b[i], e_kv_norm[i], e_wkv_b[i], e_w_out[i], mla_tabs)
        else:
            x = _odd_layer(x, mods, seq, norm_mix[l], o_w_in[i], o_q_norm[i], o_k_norm[i], o_fnet_w[i],
                           o_w_out[i], gqa_tabs)
        x = _ffn(x, mods, seq, norm_ffn[l], ffn_gate[l], ffn_up[l], ffn_down[l])

    y = _final_norm(x, final_norm)
    y_prompt = y[:n_prompt * seq].reshape(n_prompt, seq, d)
    y_sample = y[n_prompt * seq:].reshape(n_sample, seq, d)
    return (y_prompt, y_sample)
```

```python
import functools
import math

import jax
import jax.numpy as jnp
from jax import lax
from jax.experimental import pallas as pl
from jax.experimental.pallas import tpu as pltpu

F32 = jnp.float32
BF16 = jnp.bfloat16

EPS = 1e-6
ROPE_THETA = 10000.0
GRID_W = 64
POOL_WINDOWS = (2, 4, 8, 16)
MLA_NOPE = 128
MLA_ROPE = 64
MLA_V = 128
MLA_QK_PAD = 256
MLA_SHIFT_LANE = 32
GQA_HEAD_DIM = 128
GQA_GROUP = 4
FNET_GROUPS = 4
LOG2E = 1.4426950408889634

LANES = 128
SUBLANES = 8
VMEM_LIMIT_BYTES = 52 * 1024 * 1024

HIGHEST = lax.Precision.HIGHEST


def _pick(n, target, quantum=LANES):
    best = None
    t = quantum
    while t <= min(n, target):
        if n % t == 0:
            best = t
        t += quantum
    if best is None:
        return n
    return best


def _params(*sem, vmem_limit=VMEM_LIMIT_BYTES):
    return pltpu.CompilerParams(dimension_semantics=sem, vmem_limit_bytes=vmem_limit)


ROPE_SPLIT = LANES // 2


def _rot_pair(x, c, s):
    return x * c + pltpu.roll(x, ROPE_SPLIT, 1) * s


def _half_split_perm(width):
    q = width // 4
    idx = jnp.arange(width).reshape(2, 2, q)
    return idx.transpose(1, 0, 2).reshape(width)


def _mod_kernel(c_ref, w_ref, b_ref, o_ref):
    c = c_ref[...]
    act = (c / (1.0 + jnp.exp(-c))).astype(BF16)
    w = w_ref[0].astype(BF16)
    o_ref[0] = jnp.dot(act, w, preferred_element_type=F32) + b_ref[0]


def _modulation(c_pad, mod_w, mod_b):
    depth, d, n = mod_w.shape
    rows = c_pad.shape[0]
    tn = _pick(n, 512)
    return pl.pallas_call(
        _mod_kernel,
        grid=(depth, n // tn),
        in_specs=[
            pl.BlockSpec((rows, d), lambda l, j: (0, 0)),
            pl.BlockSpec((1, d, tn), lambda l, j: (l, 0, j)),
            pl.BlockSpec((1, 1, tn), lambda l, j: (l, 0, j)),
        ],
        out_specs=pl.BlockSpec((1, rows, tn), lambda l, j: (l, 0, j)),
        out_shape=jax.ShapeDtypeStruct((depth, rows, n), F32),
        compiler_params=_params("parallel", "parallel"),
        name="modulation",
    )(c_pad, mod_w, mod_b.reshape(depth, 1, n))


def _normmod_kernel(x_ref, w_ref, m_ref, o_ref, *, shift_row, scale_row):
    x = x_ref[...]
    y = x * lax.rsqrt(jnp.mean(x * x, axis=-1, keepdims=True) + EPS) * w_ref[...]
    sc = m_ref[0, scale_row:scale_row + 1, :]
    sh = m_ref[0, shift_row:shift_row + 1, :]
    o_ref[...] = (y * (1.0 + sc) + sh).astype(o_ref.dtype)


def _norm_kernel(x_ref, w_ref, o_ref):
    x = x_ref[...]
    y = x * lax.rsqrt(jnp.mean(x * x, axis=-1, keepdims=True) + EPS) * w_ref[...]
    o_ref[...] = y.astype(o_ref.dtype)


def _normmod(x, w, mods, seq, shift_row, scale_row):
    t, d = x.shape
    tm = _pick(seq, 256, SUBLANES)
    per_seq = seq // tm
    return pl.pallas_call(
        functools.partial(_normmod_kernel, shift_row=shift_row, scale_row=scale_row),
        grid=(t // tm,),
        in_specs=[
            pl.BlockSpec((tm, d), lambda i: (i, 0)),
            pl.BlockSpec((1, d), lambda i: (0, 0)),
            pl.BlockSpec((1, 6, d), lambda i: (i // per_seq, 0, 0)),
        ],
        out_specs=pl.BlockSpec((tm, d), lambda i: (i, 0)),
        out_shape=jax.ShapeDtypeStruct((t, d), BF16),
        compiler_params=_params("parallel"),
        name="normmod",
    )(x, w.reshape(1, d), mods)


def _final_norm(x, w, seq, first_seq, n_seq):
    d = x.shape[1]
    tm = _pick(seq, 256, SUBLANES)
    first_block = first_seq * (seq // tm)
    return pl.pallas_call(
        _norm_kernel,
        grid=(n_seq * seq // tm,),
        in_specs=[pl.BlockSpec((tm, d), lambda i: (first_block + i, 0)),
                  pl.BlockSpec((1, d), lambda i: (0, 0))],
        out_specs=pl.BlockSpec((tm, d), lambda i: (i, 0)),
        out_shape=jax.ShapeDtypeStruct((n_seq * seq, d), F32),
        compiler_params=_params("parallel"),
        name="final_norm",
    )(x, w.reshape(1, d))


def _mm_plain_kernel(a_ref, w_ref, o_ref):
    a = a_ref[...].astype(w_ref.dtype)
    o_ref[...] = jnp.dot(a, w_ref[...], preferred_element_type=F32).astype(o_ref.dtype)


def _matmul(a, w, out_dtype, tm_target=1024, tn_target=512):
    m, k = a.shape
    n = w.shape[1]
    tm = _pick(m, tm_target, SUBLANES)
    tn = _pick(n, tn_target)
    return pl.pallas_call(
        _mm_plain_kernel,
        grid=(m // tm, n // tn),
        in_specs=[pl.BlockSpec((tm, k), lambda i, j: (i, 0)),
                  pl.BlockSpec((k, tn), lambda i, j: (0, j))],
        out_specs=pl.BlockSpec((tm, tn), lambda i, j: (i, j)),
        out_shape=jax.ShapeDtypeStruct((m, n), out_dtype),
        compiler_params=_params("parallel", "parallel"),
        name="matmul",
    )(a, w)


def _mm_resid_kernel(*refs, n_a, gate_row):
    a_refs = refs[:n_a]
    w_refs = refs[n_a:2 * n_a]
    x_ref, m_ref, o_ref = refs[2 * n_a:]
    acc = jnp.dot(a_refs[0][...], w_refs[0][...], preferred_element_type=F32)
    for a_ref, w_ref in zip(a_refs[1:], w_refs[1:]):
        acc = acc + jnp.dot(a_ref[...], w_ref[...], preferred_element_type=F32)
    o_ref[...] = x_ref[...] + m_ref[0, gate_row:gate_row + 1, :] * acc


def _matmul_resid(a_list, w_list, x, mods, seq, gate_row, tm_target, tn_target):
    t, n = x.shape
    tm = _pick(seq, tm_target, SUBLANES)
    tn = _pick(n, tn_target)
    per_seq = seq // tm
    n_a = len(a_list)
    in_specs = [pl.BlockSpec((tm, a.shape[1]), lambda i, j: (i, 0)) for a in a_list]
    in_specs += [pl.BlockSpec((w.shape[0], tn), lambda i, j: (0, j)) for w in w_list]
    in_specs += [pl.BlockSpec((tm, tn), lambda i, j: (i, j)),
                 pl.BlockSpec((1, 6, tn), lambda i, j: (i // per_seq, 0, j))]
    return pl.pallas_call(
        functools.partial(_mm_resid_kernel, n_a=n_a, gate_row=gate_row),
        grid=(t // tm, n // tn),
        in_specs=in_specs,
        out_specs=pl.BlockSpec((tm, tn), lambda i, j: (i, j)),
        out_shape=jax.ShapeDtypeStruct((t, n), F32),
        compiler_params=_params("parallel", "parallel"),
        name="matmul_resid",
    )(*a_list, *w_list, x, mods)


def _mm_swiglu_kernel(a_ref, wg_ref, wu_ref, o_ref):
    a = a_ref[...]
    g = jnp.dot(a, wg_ref[...], preferred_element_type=F32)
    u = jnp.dot(a, wu_ref[...], preferred_element_type=F32)
    o_ref[...] = (g / (1.0 + jnp.exp(-g)) * u).astype(o_ref.dtype)


def _matmul_swiglu(a, wg, wu, tm_target=2048, tn_target=256):
    m, k = a.shape
    n = wg.shape[1]
    tm = _pick(m, tm_target, SUBLANES)
    tn = _pick(n, tn_target)
    return pl.pallas_call(
        _mm_swiglu_kernel,
        grid=(m // tm, n // tn),
        in_specs=[pl.BlockSpec((tm, k), lambda i, j: (i, 0)),
                  pl.BlockSpec((k, tn), lambda i, j: (0, j)),
                  pl.BlockSpec((k, tn), lambda i, j: (0, j))],
        out_specs=pl.BlockSpec((tm, tn), lambda i, j: (i, j)),
        out_shape=jax.ShapeDtypeStruct((m, n), BF16),
        compiler_params=_params("parallel", "parallel"),
        name="matmul_swiglu",
    )(a, wg, wu)


def _mla_q_kernel(a_ref, nw_ref, w_ref, c_ref, s_ref, o_ref, *, scale):
    a = a_ref[...]
    nrm = (a * lax.rsqrt(jnp.mean(a * a, axis=-1, keepdims=True) + EPS) * nw_ref[...]).astype(BF16)
    c, s = c_ref[...], s_ref[...]
    for h in range(o_ref.shape[1] // MLA_QK_PAD):
        lo = h * MLA_QK_PAD
        acc = jnp.dot(nrm, w_ref[:, lo:lo + MLA_QK_PAD], preferred_element_type=F32)
        o_ref[:, lo:lo + MLA_NOPE] = (acc[:, :MLA_NOPE] * scale).astype(o_ref.dtype)
        o_ref[:, lo + MLA_NOPE:lo + MLA_QK_PAD] = (_rot_pair(acc[:, MLA_NOPE:], c, s) * scale).astype(o_ref.dtype)


def _mla_q(z, col_block, q_norm, wq_pad, tabs, seq, scale):
    t = z.shape[0]
    rank, n = wq_pad.shape
    tm = _pick(seq, 512, SUBLANES)
    per_seq = seq // tm
    tab_spec = pl.BlockSpec((tm, LANES), lambda i: (i % per_seq, 0))
    return pl.pallas_call(
        functools.partial(_mla_q_kernel, scale=scale),
        grid=(t // tm,),
        in_specs=[pl.BlockSpec((tm, rank), lambda i: (i, col_block)),
                  pl.BlockSpec((1, rank), lambda i: (0, 0)),
                  pl.BlockSpec((rank, n), lambda i: (0, 0)),
                  tab_spec, tab_spec],
        out_specs=pl.BlockSpec((tm, n), lambda i: (i, 0)),
        out_shape=jax.ShapeDtypeStruct((t, n), BF16),
        compiler_params=_params("parallel"),
        name="mla_q_proj",
    )(z, q_norm.reshape(1, rank), wq_pad, *tabs)


def _mla_kv_kernel(a_ref, kr_ref, nw_ref, w_ref, c_ref, s_ref, k_ref, v_ref):
    a = a_ref[...]
    nrm = (a * lax.rsqrt(jnp.mean(a * a, axis=-1, keepdims=True) + EPS) * nw_ref[...]).astype(BF16)
    kr = _rot_pair(kr_ref[...], c_ref[...], s_ref[...])
    lane = lax.broadcasted_iota(jnp.int32, kr.shape, 1)
    kr = jnp.where(lane == MLA_SHIFT_LANE, 1.0, kr).astype(k_ref.dtype)
    ones = _ones_col(kr.shape[0], v_ref.dtype)
    width = MLA_NOPE + MLA_V
    for h in range(w_ref.shape[1] // width):
        acc = jnp.dot(nrm, w_ref[:, h * width:(h + 1) * width], preferred_element_type=F32)
        k_ref[:, h * MLA_QK_PAD:h * MLA_QK_PAD + MLA_NOPE] = acc[:, :MLA_NOPE].astype(k_ref.dtype)
        k_ref[:, h * MLA_QK_PAD + MLA_NOPE:(h + 1) * MLA_QK_PAD] = kr
        v_ref[:, h * ATTN_PAD:h * ATTN_PAD + MLA_V] = acc[:, MLA_NOPE:].astype(v_ref.dtype)
        v_ref[:, h * ATTN_PAD + MLA_V:(h + 1) * ATTN_PAD] = ones


def _mla_kv(z, kv_col_block, kr_col_block, kv_norm, wkv, tabs, seq):
    t = z.shape[0]
    rank, n = wkv.shape
    heads = n // (MLA_NOPE + MLA_V)
    tm = _pick(seq, 512, SUBLANES)
    per_seq = seq // tm
    tab_spec = pl.BlockSpec((tm, LANES), lambda i: (i % per_seq, 0))
    return pl.pallas_call(
        _mla_kv_kernel,
        grid=(t // tm,),
        in_specs=[pl.BlockSpec((tm, rank), lambda i: (i, kv_col_block)),
                  pl.BlockSpec((tm, LANES), lambda i: (i, kr_col_block)),
                  pl.BlockSpec((1, rank), lambda i: (0, 0)),
                  pl.BlockSpec((rank, n), lambda i: (0, 0)),
                  tab_spec, tab_spec],
        out_specs=[pl.BlockSpec((tm, heads * MLA_QK_PAD), lambda i: (i, 0)),
                   pl.BlockSpec((tm, heads * ATTN_PAD), lambda i: (i, 0))],
        out_shape=[jax.ShapeDtypeStruct((t, heads * MLA_QK_PAD), BF16),
                   jax.ShapeDtypeStruct((t, heads * ATTN_PAD), BF16)],
        compiler_params=_params("parallel"),
        name="mla_kv_proj",
    )(z, z, kv_norm.reshape(1, rank), wkv, *tabs)


MXU_WIDTH = 256


def _head_norm_rope(x, nw, c, s):
    y = x * lax.rsqrt(jnp.mean(x * x, axis=-1, keepdims=True) + EPS) * nw
    return _rot_pair(y, c, s)


def _gqa_qk_kernel(a_ref, w_ref, nw_ref, c_ref, s_ref, o_ref, *, scale):
    a = a_ref[...]
    nw, c, s = nw_ref[...], c_ref[...], s_ref[...]
    for p in range(o_ref.shape[1] // MXU_WIDTH):
        acc = jnp.dot(a, w_ref[:, p * MXU_WIDTH:(p + 1) * MXU_WIDTH], preferred_element_type=F32)
        for h in range(MXU_WIDTH // GQA_HEAD_DIM):
            lo = h * GQA_HEAD_DIM
            y = _head_norm_rope(acc[:, lo:lo + GQA_HEAD_DIM], nw, c, s) * scale
            o_ref[:, p * MXU_WIDTH + lo:p * MXU_WIDTH + lo + GQA_HEAD_DIM] = y.astype(o_ref.dtype)


def _gqa_qk(a, w, head_norm, tabs, seq, scale):
    t, k = a.shape
    n = w.shape[1]
    tm = _pick(seq, 1024, SUBLANES)
    tn = _pick(n, 512)
    per_seq = seq // tm
    tab_spec = pl.BlockSpec((tm, LANES), lambda i, j: (i % per_seq, 0))
    return pl.pallas_call(
        functools.partial(_gqa_qk_kernel, scale=scale),
        grid=(t // tm, n // tn),
        in_specs=[pl.BlockSpec((tm, k), lambda i, j: (i, 0)),
                  pl.BlockSpec((k, tn), lambda i, j: (0, j)),
                  pl.BlockSpec((1, GQA_HEAD_DIM), lambda i, j: (0, 0)),
                  tab_spec, tab_spec],
        out_specs=pl.BlockSpec((tm, tn), lambda i, j: (i, j)),
        out_shape=jax.ShapeDtypeStruct((t, n), BF16),
        compiler_params=_params("parallel", "parallel"),
        name="gqa_qk_proj",
    )(a, w, head_norm.reshape(1, GQA_HEAD_DIM), *tabs)


def _gqa_kv_kernel(a_ref, w_ref, nw_ref, c_ref, s_ref, k_ref, v_ref):
    a = a_ref[...]
    nw, c, s = nw_ref[...], c_ref[...], s_ref[...]
    kd = w_ref.shape[1] // 2
    ones = _ones_col(a.shape[0], k_ref.dtype)
    per_dot = MXU_WIDTH // GQA_HEAD_DIM
    for p in range(w_ref.shape[1] // MXU_WIDTH):
        acc = jnp.dot(a, w_ref[:, p * MXU_WIDTH:(p + 1) * MXU_WIDTH], preferred_element_type=F32)
        for h in range(per_dot):
            x = acc[:, h * GQA_HEAD_DIM:(h + 1) * GQA_HEAD_DIM]
            col = p * MXU_WIDTH + h * GQA_HEAD_DIM
            if col < kd:
                head = col // GQA_HEAD_DIM
                k_ref[:, head * ATTN_PAD:head * ATTN_PAD + GQA_HEAD_DIM] = (
                    _head_norm_rope(x, nw, c, s).astype(k_ref.dtype))
                k_ref[:, head * ATTN_PAD + GQA_HEAD_DIM:(head + 1) * ATTN_PAD] = ones
            else:
                head = (col - kd) // GQA_HEAD_DIM
                v_ref[:, head * ATTN_PAD:head * ATTN_PAD + GQA_HEAD_DIM] = x.astype(v_ref.dtype)
                v_ref[:, head * ATTN_PAD + GQA_HEAD_DIM:(head + 1) * ATTN_PAD] = ones


def _gqa_kv(a, w_kv, k_norm, tabs, seq):
    t, k = a.shape
    n = w_kv.shape[1]
    heads = n // (2 * GQA_HEAD_DIM)
    tm = _pick(seq, 512, SUBLANES)
    per_seq = seq // tm
    tab_spec = pl.BlockSpec((tm, LANES), lambda i: (i % per_seq, 0))
    out_spec = pl.BlockSpec((tm, heads * ATTN_PAD), lambda i: (i, 0))
    out_shape = jax.ShapeDtypeStruct((t, heads * ATTN_PAD), BF16)
    return pl.pallas_call(
        _gqa_kv_kernel,
        grid=(t // tm,),
        in_specs=[pl.BlockSpec((tm, k), lambda i: (i, 0)),
                  pl.BlockSpec((k, n), lambda i: (0, 0)),
                  pl.BlockSpec((1, GQA_HEAD_DIM), lambda i: (0, 0)),
                  tab_spec, tab_spec],
        out_specs=[out_spec, out_spec],
        out_shape=[out_shape, out_shape],
        compiler_params=_params("parallel"),
        name="gqa_kv_proj",
    )(a, w_kv, k_norm.reshape(1, GQA_HEAD_DIM), *tabs)


ATTN_PAD = 256
FAST_PATH_MIN_SUM = 2.0 ** -60
BOUND_MARGIN = 1.0 + 2.0 ** -6
NT_DIMS = (((1,), (1,)), ((), ()))
FAST_PATH_UNROLL = 8


def _ones_col(rows, dtype):
    lane = lax.broadcasted_iota(jnp.int32, (rows, LANES), 1)
    return jnp.where(lane == 0, 1.0, 0.0).astype(dtype)


def _flash_kernel(q_ref, k_ref, v_ref, o_ref, kmax_scr, acc_scr, m_scr, l_scr, *,
                  n_heads, dq, dv, shared_kv, shift_col, chunk):
    seq = k_ref.shape[0]
    tq = q_ref.shape[0]
    n_chunks = seq // chunk
    n_kv = 1 if shared_kv else n_heads

    def kv_chunk(c, hk):
        r0 = pl.multiple_of(c * chunk, chunk)
        cols = slice(hk * ATTN_PAD, (hk + 1) * ATTN_PAD)
        return k_ref[pl.ds(r0, chunk), cols], v_ref[pl.ds(r0, chunk), cols]

    @pl.when(pl.program_id(2) == 0)
    def _():
        for hk in range(n_kv):
            def norm_body(c, best):
                kf = kv_chunk(c, hk)[0].astype(F32)
                row = jnp.sum(kf * kf, axis=-1, keepdims=True)
                return jnp.maximum(best, jnp.max(row, axis=0, keepdims=True))
            best = lax.fori_loop(0, n_chunks, norm_body, jnp.zeros((1, 1), F32))
            kmax_scr[hk] = jnp.broadcast_to(jnp.sqrt(best), kmax_scr.shape[1:])

    lane = lax.broadcasted_iota(jnp.int32, (tq, LANES), 1)
    ok = None
    for h in range(n_heads):
        hk = 0 if shared_kv else h
        q = q_ref[:, h * dq:(h + 1) * dq]
        qf = q.astype(F32)
        shift = jnp.sqrt(jnp.sum(qf * qf, axis=-1, keepdims=True)) * kmax_scr[hk][0:1, 0:1] * BOUND_MARGIN
        hi = qf[:, LANES:] if dq > LANES else jnp.zeros((tq, LANES), F32)
        hi = jnp.where(lane == shift_col - LANES, -shift, hi).astype(q.dtype)
        qx = jnp.concatenate([q[:, :LANES], hi], axis=1)
        acc_scr[...] = jnp.zeros(acc_scr.shape, F32)

        def fast_body(c, carry):
            kc, vc = kv_chunk(c, hk)
            s = lax.dot_general(qx, kc, NT_DIMS, preferred_element_type=F32)
            acc_scr[...] += jnp.dot(jnp.exp2(s).astype(vc.dtype), vc, preferred_element_type=F32)
            return carry

        lax.fori_loop(0, n_chunks, fast_body, 0, unroll=min(FAST_PATH_UNROLL, n_chunks))
        acc = acc_scr[...]
        l = acc[:, dv:dv + 1]
        o_ref[:, h * dv:(h + 1) * dv] = (acc[:, :dv] / l).astype(o_ref.dtype)
        head_ok = jnp.min(l) >= FAST_PATH_MIN_SUM
        ok = head_ok if ok is None else jnp.logical_and(ok, head_ok)

    @pl.when(jnp.logical_not(ok))
    def _():
        for h in range(n_heads):
            hk = 0 if shared_kv else h
            q = q_ref[:, h * dq:(h + 1) * dq]
            m_scr[...] = jnp.full(m_scr.shape, -jnp.inf, F32)
            l_scr[...] = jnp.zeros(l_scr.shape, F32)
            acc_scr[...] = jnp.zeros(acc_scr.shape, F32)

            def exact_body(c, carry):
                kc, vc = kv_chunk(c, hk)
                s = lax.dot_general(q, kc[:, :dq], NT_DIMS, preferred_element_type=F32)
                m_prev = m_scr[:, 0:1]
                m_new = jnp.maximum(m_prev, jnp.max(s, axis=-1, keepdims=True))
                alpha = jnp.exp2(m_prev - m_new)
                p = jnp.exp2(s - m_new)
                l_new = alpha * l_scr[:, 0:1] + jnp.sum(p, axis=-1, keepdims=True)
                acc_scr[...] = alpha * acc_scr[...] + jnp.dot(p.astype(vc.dtype), vc, preferred_element_type=F32)
                m_scr[...] = jnp.broadcast_to(m_new, m_scr.shape)
                l_scr[...] = jnp.broadcast_to(l_new, l_scr.shape)
                return carry

            lax.fori_loop(0, n_chunks, exact_body, 0)
            o_ref[:, h * dv:(h + 1) * dv] = (acc_scr[:, :dv] / l_scr[:, 0:1]).astype(o_ref.dtype)


def _flash(q, k, v, seq, n_groups, n_heads, dq, shared_kv, shift_col, tq_target=512, chunk_target=1024):
    t = q.shape[0]
    n_seq = t // seq
    dv = ATTN_PAD // 2
    tq = _pick(seq, tq_target, SUBLANES)
    chunk = _pick(seq, chunk_target, SUBLANES)
    qb = seq // tq
    n_kv = 1 if shared_kv else n_heads
    return pl.pallas_call(
        functools.partial(_flash_kernel, n_heads=n_heads, dq=dq, dv=dv, shared_kv=shared_kv,
                          shift_col=shift_col, chunk=chunk),
        grid=(n_seq, n_groups, qb),
        in_specs=[pl.BlockSpec((tq, n_heads * dq), lambda b, g, i: (b * qb + i, g)),
                  pl.BlockSpec((seq, n_kv * ATTN_PAD), lambda b, g, i: (b, g)),
                  pl.BlockSpec((seq, n_kv * ATTN_PAD), lambda b, g, i: (b, g))],
        out_specs=pl.BlockSpec((tq, n_heads * dv), lambda b, g, i: (b * qb + i, g)),
        out_shape=jax.ShapeDtypeStruct((t, n_groups * n_heads * dv), BF16),
        scratch_shapes=[pltpu.VMEM((n_kv, SUBLANES, LANES), F32),
                        pltpu.VMEM((tq, ATTN_PAD), F32),
                        pltpu.VMEM((tq, LANES), F32),
                        pltpu.VMEM((tq, LANES), F32)],
        compiler_params=_params("parallel", "parallel", "arbitrary"),
        name="flash_attention",
    )(q, k, v)


POOL_HALO = 16


def _pool_kernel(prev_ref, cur_ref, next_ref, w_ref, s_ref, o_ref, ext_ref, *, seq, group):
    tm = cur_ref.shape[0]
    ext_ref[0:POOL_HALO, :] = prev_ref[...]
    ext_ref[POOL_HALO:POOL_HALO + tm, :] = cur_ref[...]
    ext_ref[POOL_HALO + tm:, :] = next_ref[...]
    pos = (pl.program_id(0) * tm) % seq + lax.broadcasted_iota(jnp.int32, (tm, 1), 0)
    for gi, win in enumerate(POOL_WINDOWS):
        cols = slice(gi * group, (gi + 1) * group)
        total = jnp.zeros((tm, group), F32)
        count = jnp.zeros((tm, 1), F32)
        for off in range(-(win // 2), win - win // 2):
            valid = jnp.logical_and(pos + off >= 0, pos + off < seq)
            term = ext_ref[POOL_HALO + off:POOL_HALO + off + tm, cols]
            total = total + jnp.where(valid, term, 0.0)
            count = count + valid.astype(F32)
        diff = (total / count - cur_ref[:, cols]).astype(BF16)
        y = jnp.dot(diff, w_ref[gi], preferred_element_type=F32) * s_ref[:, cols]
        o_ref[:, cols] = y.astype(o_ref.dtype)


def _pool(z, pool_w, pool_scale, seq):
    t = z.shape[0]
    n_groups, group, _ = pool_w.shape
    width = n_groups * group
    tm = _pick(seq, 256, POOL_HALO)
    halo_blocks = tm // POOL_HALO
    last_halo = t // POOL_HALO - 1
    return pl.pallas_call(
        functools.partial(_pool_kernel, seq=seq, group=group),
        grid=(t // tm,),
        in_specs=[
            pl.BlockSpec((POOL_HALO, width), lambda i: (jnp.maximum(i * halo_blocks - 1, 0), 0)),
            pl.BlockSpec((tm, width), lambda i: (i, 0)),
            pl.BlockSpec((POOL_HALO, width), lambda i: (jnp.minimum((i + 1) * halo_blocks, last_halo), 0)),
            pl.BlockSpec((n_groups, group, group), lambda i: (0, 0, 0)),
            pl.BlockSpec((1, width), lambda i: (0, 0)),
        ],
        out_specs=pl.BlockSpec((tm, width), lambda i: (i, 0)),
        out_shape=jax.ShapeDtypeStruct((t, width), BF16),
        scratch_shapes=[pltpu.VMEM((tm + 2 * POOL_HALO, width), F32)],
        compiler_params=_params("parallel"),
        name="pool_mixer",
    )(z, z, z, pool_w, pool_scale.reshape(1, width))


FFT_INNER = 128


def _fnet_chan_kernel(u_ref, m_ref, o_ref):
    gw = u_ref.shape[1]
    res = jnp.dot(u_ref[...], m_ref[...], precision=HIGHEST, preferred_element_type=F32)
    o_ref[0, 0] = res[:, :gw]
    o_ref[0, 1] = res[:, gw:]


def _fnet_outer_kernel(v_ref, m_ref, o_ref):
    two, n1, rows, width = v_ref.shape[1:]
    m = m_ref[...]
    for r in range(rows):
        x = v_ref[0, :, :, r, :].reshape(two * n1, width)
        y = jnp.dot(m, x, precision=HIGHEST, preferred_element_type=F32)
        o_ref[0, :, :, r, :] = y.reshape(two, n1, width)


def _fnet_inner_kernel(a_ref, g_ref, o_ref, *, norm):
    two, group, inner, width = a_ref.shape[1:]
    for r in range(group):
        x = a_ref[0, :, r].reshape(two * inner, width)
        y = jnp.dot(g_ref[r], x, precision=HIGHEST, preferred_element_type=F32)
        o_ref[0, :, r, :] = (y * norm).astype(o_ref.dtype)


def _fnet_tables(seq, gw):
    n1 = seq // FFT_INNER
    ch = jnp.arange(gw, dtype=jnp.int32)
    ang = (2.0 * math.pi / gw) * ((ch[:, None] * ch[None, :]) % gw).astype(F32)
    chan = jnp.concatenate([jnp.cos(ang), -jnp.sin(ang)], axis=1)
    i1 = jnp.arange(n1, dtype=jnp.int32)
    ang = (2.0 * math.pi / n1) * ((i1[:, None] * i1[None, :]) % n1).astype(F32)
    c, s = jnp.cos(ang), jnp.sin(ang)
    outer = jnp.concatenate([jnp.concatenate([c, s], axis=1),
                             jnp.concatenate([-s, c], axis=1)], axis=0)
    k2 = jnp.arange(FFT_INNER, dtype=jnp.int32)
    kk = i1[:, None, None] + n1 * k2[None, :, None]
    ang = (2.0 * math.pi / seq) * ((kk * k2[None, None, :]) % seq).astype(F32)
    inner = jnp.concatenate([jnp.cos(ang), jnp.sin(ang)], axis=2)
    return chan, outer, inner


def _fourier(u, seq):
    t, width = u.shape
    n_seq = t // seq
    gw = width // FNET_GROUPS
    n1 = seq // FFT_INNER
    chan, outer, inner = _fnet_tables(seq, gw)
    tm = _pick(seq, 512, SUBLANES)
    per_seq = seq // tm
    v = pl.pallas_call(
        _fnet_chan_kernel,
        grid=(t // tm, FNET_GROUPS),
        in_specs=[pl.BlockSpec((tm, gw), lambda i, g: (i, g)),
                  pl.BlockSpec((gw, 2 * gw), lambda i, g: (0, 0))],
        out_specs=pl.BlockSpec((1, 2, tm, gw), lambda i, g: (i // per_seq, 0, i % per_seq, g)),
        out_shape=jax.ShapeDtypeStruct((n_seq, 2, seq, width), F32),
        compiler_params=_params("parallel", "parallel"),
        name="fnet_channel_dft",
    )(u, chan)
    outer_spec = pl.BlockSpec((1, 2, n1, SUBLANES, width), lambda b, j: (b, 0, 0, j, 0))
    a = pl.pallas_call(
        _fnet_outer_kernel,
        grid=(n_seq, FFT_INNER // SUBLANES),
        in_specs=[outer_spec, pl.BlockSpec((2 * n1, 2 * n1), lambda b, j: (0, 0))],
        out_specs=outer_spec,
        out_shape=jax.ShapeDtypeStruct((n_seq, 2, n1, FFT_INNER, width), F32),
        compiler_params=_params("parallel", "parallel"),
        name="fnet_outer_dft",
    )(v.reshape(n_seq, 2, n1, FFT_INNER, width), outer)
    group = min(SUBLANES, n1)
    f = pl.pallas_call(
        functools.partial(_fnet_inner_kernel, norm=1.0 / math.sqrt(seq * gw)),
        grid=(n_seq, n1 // group),
        in_specs=[pl.BlockSpec((1, 2, group, FFT_INNER, width), lambda b, k: (b, 0, k, 0, 0)),
                  pl.BlockSpec((group, FFT_INNER, 2 * FFT_INNER), lambda b, k: (k, 0, 0))],
        out_specs=pl.BlockSpec((1, FFT_INNER, group, width), lambda b, k: (b, 0, k, 0)),
        out_shape=jax.ShapeDtypeStruct((n_seq, FFT_INNER, n1, width), F32),
        compiler_params=_params("parallel", "parallel"),
        name="fnet_inner_dft",
    )(a, inner)
    return f.reshape(t, width)


def _rope_freqs(dim):
    return ROPE_THETA ** (-jnp.arange(0, dim, 2, dtype=F32) / dim)


def _pair_tables(ang_a, ang_b):
    ca, sa = jnp.cos(ang_a), jnp.sin(ang_a)
    zero = jnp.zeros_like(ca)
    if ang_b is None:
        cb, sb = zero, zero
    else:
        cb, sb = jnp.cos(ang_b), jnp.sin(ang_b)
    cos = jnp.concatenate([ca, cb, ca, cb], axis=1)
    sin = jnp.concatenate([-sa, -sb, sa, sb], axis=1)
    return cos, sin


def _pad_cols(w, width):
    return jnp.pad(w, ((0, 0), (0, width - w.shape[1])))


def _even_layer(x, mods, seq, norm_w, w_in, pool_w, pool_scale, q_norm, wq_b, kv_norm, wkv_b, w_out, tabs):
    pool_width = pool_scale.shape[0]
    q_rank = q_norm.shape[0]
    kv_rank = kv_norm.shape[0]
    heads = wq_b.shape[1] // (MLA_NOPE + MLA_ROPE)
    half = MLA_ROPE // 2
    kr_lo = pool_width + q_rank + kv_rank
    w_in_p = jnp.concatenate([w_in[:, :kr_lo], _pad_cols(w_in[:, kr_lo:kr_lo + half], ROPE_SPLIT),
                              _pad_cols(w_in[:, kr_lo + half:], ROPE_SPLIT)], axis=1)
    w_in_p = _pad_cols(w_in_p, -(-w_in_p.shape[1] // MXU_WIDTH) * MXU_WIDTH).astype(BF16)
    wq = wq_b.reshape(q_rank, heads, MLA_NOPE + MLA_ROPE)
    spare = jnp.zeros((q_rank, heads, ROPE_SPLIT - half), wq.dtype)
    wq_pad = jnp.concatenate([wq[..., :MLA_NOPE], wq[..., MLA_NOPE:MLA_NOPE + half], spare,
                              wq[..., MLA_NOPE + half:], spare], axis=-1)
    wq_pad = wq_pad.reshape(q_rank, heads * MLA_QK_PAD).astype(BF16)

    h = _normmod(x, norm_w, mods, seq, shift_row=0, scale_row=1)
    z = _matmul(h, w_in_p, F32, tm_target=1024, tn_target=768)
    a = _pool(z, pool_w.astype(BF16), pool_scale, seq)
    scale = (MLA_NOPE + MLA_ROPE) ** -0.5 * LOG2E
    q = _mla_q(z, pool_width // q_rank, q_norm, wq_pad, tabs, seq, scale)
    k, v = _mla_kv(z, (pool_width + q_rank) // kv_rank, (pool_width + q_rank + kv_rank) // LANES,
                   kv_norm, wkv_b.astype(BF16), tabs, seq)
    attn = _flash(q, k, v, seq, n_groups=heads // 2, n_heads=2, dq=MLA_QK_PAD, shared_kv=False,
                  shift_col=MLA_NOPE + MLA_SHIFT_LANE)
    w_out = w_out.astype(BF16)
    return _matmul_resid([a, attn], [w_out[:pool_width], w_out[pool_width:]], x, mods, seq,
                         gate_row=2, tm_target=1024, tn_target=512)


def _odd_layer(x, mods, seq, norm_w, w_in, q_norm, k_norm, fnet_w, w_out, tabs):
    fnet_width = fnet_w.shape[0]
    qkv_width = w_in.shape[1] - fnet_width
    kd = qkv_width // (GQA_GROUP + 2)
    qd = GQA_GROUP * kd
    w_in = w_in.astype(BF16)
    perm = _half_split_perm(GQA_HEAD_DIM)
    d_in = w_in.shape[0]
    w_q = w_in[:, :qd].reshape(d_in, qd // GQA_HEAD_DIM, GQA_HEAD_DIM)[:, :, perm].reshape(d_in, qd)
    w_k = w_in[:, qd:qd + kd].reshape(d_in, kd // GQA_HEAD_DIM, GQA_HEAD_DIM)[:, :, perm].reshape(d_in, kd)
    w_kv = jnp.concatenate([w_k, w_in[:, qd + kd:qd + 2 * kd]], axis=1)

    h = _normmod(x, norm_w, mods, seq, shift_row=0, scale_row=1)
    q = _gqa_qk(h, w_q, q_norm[perm], tabs, seq, GQA_HEAD_DIM ** -0.5 * LOG2E)
    k, v = _gqa_kv(h, w_kv, k_norm[perm], tabs, seq)
    u = _matmul(h, w_in[:, qd + 2 * kd:], F32)
    attn = _flash(q, k, v, seq, n_groups=kd // GQA_HEAD_DIM, n_heads=GQA_GROUP,
                  dq=GQA_HEAD_DIM, shared_kv=True, shift_col=GQA_HEAD_DIM)
    f = _fourier(u, seq)
    ff = _matmul(f, fnet_w.astype(BF16), BF16)
    w_out = w_out.astype(BF16)
    return _matmul_resid([attn, ff], [w_out[:qd], w_out[qd:]], x, mods, seq,
                         gate_row=2, tm_target=1024, tn_target=512)


def _ffn(x, mods, seq, norm_w, w_gate, w_up, w_down):
    h = _normmod(x, norm_w, mods, seq, shift_row=3, scale_row=4)
    hid = _matmul_swiglu(h, w_gate.astype(BF16), w_up.astype(BF16))
    return _matmul_resid([hid], [w_down.astype(BF16)], x, mods, seq,
                         gate_row=5, tm_target=512, tn_target=512)


def kernel(x_prompt, x_sample, c_prompt, c_sample, mod_w, mod_b, norm_mix, norm_ffn, e_w_in, e_pool_w, e_pool_scale, e_q_norm, e_wq_b, e_kv_norm, e_wkv_b, e_w_out, o_w_in, o_q_norm, o_k_norm, o_fnet_w, o_w_out, ffn_gate, ffn_up, ffn_down, final_norm):
    n_prompt, seq, d = x_prompt.shape
    n_sample = x_sample.shape[0]
    assert x_sample.shape[1] == seq and seq % GRID_W == 0 and seq % FFT_INNER == 0
    n_seq = n_prompt + n_sample
    depth = mod_w.shape[0]

    x = jnp.concatenate([x_prompt.reshape(n_prompt * seq, d), x_sample.reshape(n_sample * seq, d)], axis=0)
    c = jnp.concatenate([c_prompt, c_sample], axis=0)
    c_pad = jnp.pad(c, ((0, -n_seq % SUBLANES), (0, 0)))
    mods_all = _modulation(c_pad, mod_w, mod_b).reshape(depth, c_pad.shape[0], 6, d)

    pos = jnp.arange(seq)
    ang_1d = pos.astype(F32)[:, None] * _rope_freqs(MLA_ROPE)[None, :]
    ang_row = (pos // GRID_W).astype(F32)[:, None] * _rope_freqs(GQA_HEAD_DIM // 2)[None, :]
    ang_col = (pos % GRID_W).astype(F32)[:, None] * _rope_freqs(GQA_HEAD_DIM // 2)[None, :]
    mla_tabs = _pair_tables(ang_1d, None)
    gqa_tabs = _pair_tables(ang_row, ang_col)

    for l in range(depth):
        mods = mods_all[l]
        i = l // 2
        if l % 2 == 0:
            x = _even_layer(x, mods, seq, norm_mix[l], e_w_in[i], e_pool_w[i], e_pool_scale[i], e_q_norm[i],
                            e_wq_b[i], e_kv_norm[i], e_wkv_b[i], e_w_out[i], mla_tabs)
        else:
            x = _odd_layer(x, mods, seq, norm_mix[l], o_w_in[i], o_q_norm[i], o_k_norm[i], o_fnet_w[i],
                           o_w_out[i], gqa_tabs)
        x = _ffn(x, mods, seq, norm_ffn[l], ffn_gate[l], ffn_up[l], ffn_down[l])

    y_prompt = _final_norm(x, final_norm, seq, 0, n_prompt).reshape(n_prompt, seq, d)
    y_sample = _final_norm(x, final_norm, seq, n_prompt, n_sample).reshape(n_sample, seq, d)
    return (y_prompt, y_sample)
```

```python
import functools
import math

import jax
import jax.numpy as jnp
from jax import lax
from jax.experimental import pallas as pl
from jax.experimental.pallas import tpu as pltpu

F32 = jnp.float32
BF16 = jnp.bfloat16

EPS = 1e-6
ROPE_THETA = 10000.0
GRID_W = 64
POOL_WINDOWS = (2, 4, 8, 16)
MLA_NOPE = 128
MLA_ROPE = 64
MLA_V = 128
MLA_QK_PAD = 256
MLA_SHIFT_LANE = 32
GQA_HEAD_DIM = 128
GQA_GROUP = 4
FNET_GROUPS = 4
LOG2E = 1.4426950408889634

LANES = 128
SUBLANES = 8
VMEM_LIMIT_BYTES = 52 * 1024 * 1024

HIGHEST = lax.Precision.HIGHEST


def _pick(n, target, quantum=LANES):
    best = None
    t = quantum
    while t <= min(n, target):
        if n % t == 0:
            best = t
        t += quantum
    if best is None:
        return n
    return best


def _params(*sem, vmem_limit=VMEM_LIMIT_BYTES):
    return pltpu.CompilerParams(dimension_semantics=sem, vmem_limit_bytes=vmem_limit)


ROPE_SPLIT = LANES // 2


def _rot_pair(x, c, s):
    return x * c + pltpu.roll(x, ROPE_SPLIT, 1) * s


def _half_split_perm(width):
    q = width // 4
    idx = jnp.arange(width).reshape(2, 2, q)
    return idx.transpose(1, 0, 2).reshape(width)


def _mod_kernel(c_ref, w_ref, b_ref, o_ref):
    c = c_ref[...]
    act = (c / (1.0 + jnp.exp(-c))).astype(BF16)
    w = w_ref[0].astype(BF16)
    o_ref[0] = jnp.dot(act, w, preferred_element_type=F32) + b_ref[0]


def _modulation(c_pad, mod_w, mod_b):
    depth, d, n = mod_w.shape
    rows = c_pad.shape[0]
    tn = _pick(n, 512)
    return pl.pallas_call(
        _mod_kernel,
        grid=(depth, n // tn),
        in_specs=[
            pl.BlockSpec((rows, d), lambda l, j: (0, 0)),
            pl.BlockSpec((1, d, tn), lambda l, j: (l, 0, j)),
            pl.BlockSpec((1, 1, tn), lambda l, j: (l, 0, j)),
        ],
        out_specs=pl.BlockSpec((1, rows, tn), lambda l, j: (l, 0, j)),
        out_shape=jax.ShapeDtypeStruct((depth, rows, n), F32),
        compiler_params=_params("parallel", "parallel"),
        name="modulation",
    )(c_pad, mod_w, mod_b.reshape(depth, 1, n))


def _normmod_kernel(x_ref, w_ref, m_ref, o_ref, *, shift_row, scale_row):
    x = x_ref[...]
    y = x * lax.rsqrt(jnp.mean(x * x, axis=-1, keepdims=True) + EPS) * w_ref[...]
    sc = m_ref[0, scale_row:scale_row + 1, :]
    sh = m_ref[0, shift_row:shift_row + 1, :]
    o_ref[...] = (y * (1.0 + sc) + sh).astype(o_ref.dtype)


def _norm_kernel(x_ref, w_ref, o_ref):
    x = x_ref[...]
    y = x * lax.rsqrt(jnp.mean(x * x, axis=-1, keepdims=True) + EPS) * w_ref[...]
    o_ref[...] = y.astype(o_ref.dtype)


def _normmod(x, w, mods, seq, shift_row, scale_row):
    t, d = x.shape
    tm = _pick(seq, 256, SUBLANES)
    per_seq = seq // tm
    return pl.pallas_call(
        functools.partial(_normmod_kernel, shift_row=shift_row, scale_row=scale_row),
        grid=(t // tm,),
        in_specs=[
            pl.BlockSpec((tm, d), lambda i: (i, 0)),
            pl.BlockSpec((1, d), lambda i: (0, 0)),
            pl.BlockSpec((1, 6, d), lambda i: (i // per_seq, 0, 0)),
        ],
        out_specs=pl.BlockSpec((tm, d), lambda i: (i, 0)),
        out_shape=jax.ShapeDtypeStruct((t, d), BF16),
        compiler_params=_params("parallel"),
        name="normmod",
    )(x, w.reshape(1, d), mods)


def _final_norm(x, w, seq, first_seq, n_seq):
    d = x.shape[1]
    tm = _pick(seq, 256, SUBLANES)
    first_block = first_seq * (seq // tm)
    return pl.pallas_call(
        _norm_kernel,
        grid=(n_seq * seq // tm,),
        in_specs=[pl.BlockSpec((tm, d), lambda i: (first_block + i, 0)),
                  pl.BlockSpec((1, d), lambda i: (0, 0))],
        out_specs=pl.BlockSpec((tm, d), lambda i: (i, 0)),
        out_shape=jax.ShapeDtypeStruct((n_seq * seq, d), F32),
        compiler_params=_params("parallel"),
        name="final_norm",
    )(x, w.reshape(1, d))


def _mm_plain_kernel(a_ref, w_ref, o_ref):
    a = a_ref[...].astype(w_ref.dtype)
    o_ref[...] = jnp.dot(a, w_ref[...], preferred_element_type=F32).astype(o_ref.dtype)


def _matmul(a, w, out_dtype, tm_target=1024, tn_target=512):
    m, k = a.shape
    n = w.shape[1]
    tm = _pick(m, tm_target, SUBLANES)
    tn = _pick(n, tn_target)
    return pl.pallas_call(
        _mm_plain_kernel,
        grid=(m // tm, n // tn),
        in_specs=[pl.BlockSpec((tm, k), lambda i, j: (i, 0)),
                  pl.BlockSpec((k, tn), lambda i, j: (0, j))],
        out_specs=pl.BlockSpec((tm, tn), lambda i, j: (i, j)),
        out_shape=jax.ShapeDtypeStruct((m, n), out_dtype),
        compiler_params=_params("parallel", "parallel"),
        name="matmul",
    )(a, w)


def _mm_resid_kernel(*refs, n_a, gate_row):
    a_refs = refs[:n_a]
    w_refs = refs[n_a:2 * n_a]
    x_ref, m_ref, o_ref = refs[2 * n_a:]
    acc = jnp.dot(a_refs[0][...], w_refs[0][...], preferred_element_type=F32)
    for a_ref, w_ref in zip(a_refs[1:], w_refs[1:]):
        acc = acc + jnp.dot(a_ref[...], w_ref[...], preferred_element_type=F32)
    o_ref[...] = x_ref[...] + m_ref[0, gate_row:gate_row + 1, :] * acc


def _matmul_resid(a_list, w_list, x, mods, seq, gate_row, tm_target, tn_target):
    t, n = x.shape
    tm = _pick(seq, tm_target, SUBLANES)
    tn = _pick(n, tn_target)
    per_seq = seq // tm
    n_a = len(a_list)
    in_specs = [pl.BlockSpec((tm, a.shape[1]), lambda i, j: (i, 0)) for a in a_list]
    in_specs += [pl.BlockSpec((w.shape[0], tn), lambda i, j: (0, j)) for w in w_list]
    in_specs += [pl.BlockSpec((tm, tn), lambda i, j: (i, j)),
                 pl.BlockSpec((1, 6, tn), lambda i, j: (i // per_seq, 0, j))]
    return pl.pallas_call(
        functools.partial(_mm_resid_kernel, n_a=n_a, gate_row=gate_row),
        grid=(t // tm, n // tn),
        in_specs=in_specs,
        out_specs=pl.BlockSpec((tm, tn), lambda i, j: (i, j)),
        out_shape=jax.ShapeDtypeStruct((t, n), F32),
        compiler_params=_params("parallel", "parallel"),
        name="matmul_resid",
    )(*a_list, *w_list, x, mods)


def _mm_swiglu_kernel(a_ref, wg_ref, wu_ref, o_ref):
    a = a_ref[...]
    g = jnp.dot(a, wg_ref[...], preferred_element_type=F32)
    u = jnp.dot(a, wu_ref[...], preferred_element_type=F32)
    o_ref[...] = (g / (1.0 + jnp.exp(-g)) * u).astype(o_ref.dtype)


def _matmul_swiglu(a, wg, wu, tm_target=2048, tn_target=256):
    m, k = a.shape
    n = wg.shape[1]
    tm = _pick(m, tm_target, SUBLANES)
    tn = _pick(n, tn_target)
    return pl.pallas_call(
        _mm_swiglu_kernel,
        grid=(m // tm, n // tn),
        in_specs=[pl.BlockSpec((tm, k), lambda i, j: (i, 0)),
                  pl.BlockSpec((k, tn), lambda i, j: (0, j)),
                  pl.BlockSpec((k, tn), lambda i, j: (0, j))],
        out_specs=pl.BlockSpec((tm, tn), lambda i, j: (i, j)),
        out_shape=jax.ShapeDtypeStruct((m, n), BF16),
        compiler_params=_params("parallel", "parallel"),
        name="matmul_swiglu",
    )(a, wg, wu)


def _mla_q_kernel(a_ref, nw_ref, w_ref, c_ref, s_ref, o_ref, *, scale):
    a = a_ref[...]
    nrm = (a * lax.rsqrt(jnp.mean(a * a, axis=-1, keepdims=True) + EPS) * nw_ref[...]).astype(BF16)
    c, s = c_ref[...], s_ref[...]
    for h in range(o_ref.shape[1] // MLA_QK_PAD):
        lo = h * MLA_QK_PAD
        acc = jnp.dot(nrm, w_ref[:, lo:lo + MLA_QK_PAD], preferred_element_type=F32)
        o_ref[:, lo:lo + MLA_NOPE] = (acc[:, :MLA_NOPE] * scale).astype(o_ref.dtype)
        o_ref[:, lo + MLA_NOPE:lo + MLA_QK_PAD] = (_rot_pair(acc[:, MLA_NOPE:], c, s) * scale).astype(o_ref.dtype)


def _mla_q(z, col_block, q_norm, wq_pad, tabs, seq, scale):
    t = z.shape[0]
    rank, n = wq_pad.shape
    tm = _pick(seq, 512, SUBLANES)
    per_seq = seq // tm
    tab_spec = pl.BlockSpec((tm, LANES), lambda i: (i % per_seq, 0))
    return pl.pallas_call(
        functools.partial(_mla_q_kernel, scale=scale),
        grid=(t // tm,),
        in_specs=[pl.BlockSpec((tm, rank), lambda i: (i, col_block)),
                  pl.BlockSpec((1, rank), lambda i: (0, 0)),
                  pl.BlockSpec((rank, n), lambda i: (0, 0)),
                  tab_spec, tab_spec],
        out_specs=pl.BlockSpec((tm, n), lambda i: (i, 0)),
        out_shape=jax.ShapeDtypeStruct((t, n), BF16),
        compiler_params=_params("parallel"),
        name="mla_q_proj",
    )(z, q_norm.reshape(1, rank), wq_pad, *tabs)


def _mla_kv_kernel(a_ref, kr_ref, nw_ref, w_ref, c_ref, s_ref, k_ref, v_ref):
    a = a_ref[...]
    nrm = (a * lax.rsqrt(jnp.mean(a * a, axis=-1, keepdims=True) + EPS) * nw_ref[...]).astype(BF16)
    kr = _rot_pair(kr_ref[...], c_ref[...], s_ref[...])
    lane = lax.broadcasted_iota(jnp.int32, kr.shape, 1)
    kr = jnp.where(lane == MLA_SHIFT_LANE, 1.0, kr).astype(k_ref.dtype)
    ones = _ones_col(kr.shape[0], v_ref.dtype)
    width = MLA_NOPE + MLA_V
    for h in range(w_ref.shape[1] // width):
        acc = jnp.dot(nrm, w_ref[:, h * width:(h + 1) * width], preferred_element_type=F32)
        k_ref[:, h * MLA_QK_PAD:h * MLA_QK_PAD + MLA_NOPE] = acc[:, :MLA_NOPE].astype(k_ref.dtype)
        k_ref[:, h * MLA_QK_PAD + MLA_NOPE:(h + 1) * MLA_QK_PAD] = kr
        v_ref[:, h * ATTN_PAD:h * ATTN_PAD + MLA_V] = acc[:, MLA_NOPE:].astype(v_ref.dtype)
        v_ref[:, h * ATTN_PAD + MLA_V:(h + 1) * ATTN_PAD] = ones


def _mla_kv(z, kv_col_block, kr_col_block, kv_norm, wkv, tabs, seq):
    t = z.shape[0]
    rank, n = wkv.shape
    heads = n // (MLA_NOPE + MLA_V)
    tm = _pick(seq, 512, SUBLANES)
    per_seq = seq // tm
    tab_spec = pl.BlockSpec((tm, LANES), lambda i: (i % per_seq, 0))
    return pl.pallas_call(
        _mla_kv_kernel,
        grid=(t // tm,),
        in_specs=[pl.BlockSpec((tm, rank), lambda i: (i, kv_col_block)),
                  pl.BlockSpec((tm, LANES), lambda i: (i, kr_col_block)),
                  pl.BlockSpec((1, rank), lambda i: (0, 0)),
                  pl.BlockSpec((rank, n), lambda i: (0, 0)),
                  tab_spec, tab_spec],
        out_specs=[pl.BlockSpec((tm, heads * MLA_QK_PAD), lambda i: (i, 0)),
                   pl.BlockSpec((tm, heads * ATTN_PAD), lambda i: (i, 0))],
        out_shape=[jax.ShapeDtypeStruct((t, heads * MLA_QK_PAD), BF16),
                   jax.ShapeDtypeStruct((t, heads * ATTN_PAD), BF16)],
        compiler_params=_params("parallel"),
        name="mla_kv_proj",
    )(z, z, kv_norm.reshape(1, rank), wkv, *tabs)


MXU_WIDTH = 256


def _head_norm_rope(x, nw, c, s):
    y = x * lax.rsqrt(jnp.mean(x * x, axis=-1, keepdims=True) + EPS) * nw
    return _rot_pair(y, c, s)


def _gqa_qk_kernel(a_ref, w_ref, nw_ref, c_ref, s_ref, o_ref, *, scale):
    a = a_ref[...]
    nw, c, s = nw_ref[...], c_ref[...], s_ref[...]
    for p in range(o_ref.shape[1] // MXU_WIDTH):
        acc = jnp.dot(a, w_ref[:, p * MXU_WIDTH:(p + 1) * MXU_WIDTH], preferred_element_type=F32)
        for h in range(MXU_WIDTH // GQA_HEAD_DIM):
            lo = h * GQA_HEAD_DIM
            y = _head_norm_rope(acc[:, lo:lo + GQA_HEAD_DIM], nw, c, s) * scale
            o_ref[:, p * MXU_WIDTH + lo:p * MXU_WIDTH + lo + GQA_HEAD_DIM] = y.astype(o_ref.dtype)


def _gqa_qk(a, w, head_norm, tabs, seq, scale):
    t, k = a.shape
    n = w.shape[1]
    tm = _pick(seq, 1024, SUBLANES)
    tn = _pick(n, 512)
    per_seq = seq // tm
    tab_spec = pl.BlockSpec((tm, LANES), lambda i, j: (i % per_seq, 0))
    return pl.pallas_call(
        functools.partial(_gqa_qk_kernel, scale=scale),
        grid=(t // tm, n // tn),
        in_specs=[pl.BlockSpec((tm, k), lambda i, j: (i, 0)),
                  pl.BlockSpec((k, tn), lambda i, j: (0, j)),
                  pl.BlockSpec((1, GQA_HEAD_DIM), lambda i, j: (0, 0)),
                  tab_spec, tab_spec],
        out_specs=pl.BlockSpec((tm, tn), lambda i, j: (i, j)),
        out_shape=jax.ShapeDtypeStruct((t, n), BF16),
        compiler_params=_params("parallel", "parallel"),
        name="gqa_qk_proj",
    )(a, w, head_norm.reshape(1, GQA_HEAD_DIM), *tabs)


def _gqa_kv_kernel(a_ref, w_ref, nw_ref, c_ref, s_ref, k_ref, v_ref):
    a = a_ref[...]
    nw, c, s = nw_ref[...], c_ref[...], s_ref[...]
    kd = w_ref.shape[1] // 2
    ones = _ones_col(a.shape[0], k_ref.dtype)
    per_dot = MXU_WIDTH // GQA_HEAD_DIM
    for p in range(w_ref.shape[1] // MXU_WIDTH):
        acc = jnp.dot(a, w_ref[:, p * MXU_WIDTH:(p + 1) * MXU_WIDTH], preferred_element_type=F32)
        for h in range(per_dot):
            x = acc[:, h * GQA_HEAD_DIM:(h + 1) * GQA_HEAD_DIM]
            col = p * MXU_WIDTH + h * GQA_HEAD_DIM
            if col < kd:
                head = col // GQA_HEAD_DIM
                k_ref[:, head * ATTN_PAD:head * ATTN_PAD + GQA_HEAD_DIM] = (
                    _head_norm_rope(x, nw, c, s).astype(k_ref.dtype))
                k_ref[:, head * ATTN_PAD + GQA_HEAD_DIM:(head + 1) * ATTN_PAD] = ones
            else:
                head = (col - kd) // GQA_HEAD_DIM
                v_ref[:, head * ATTN_PAD:head * ATTN_PAD + GQA_HEAD_DIM] = x.astype(v_ref.dtype)
                v_ref[:, head * ATTN_PAD + GQA_HEAD_DIM:(head + 1) * ATTN_PAD] = ones


def _gqa_kv(a, w_kv, k_norm, tabs, seq):
    t, k = a.shape
    n = w_kv.shape[1]
    heads = n // (2 * GQA_HEAD_DIM)
    tm = _pick(seq, 512, SUBLANES)
    per_seq = seq // tm
    tab_spec = pl.BlockSpec((tm, LANES), lambda i: (i % per_seq, 0))
    out_spec = pl.BlockSpec((tm, heads * ATTN_PAD), lambda i: (i, 0))
    out_shape = jax.ShapeDtypeStruct((t, heads * ATTN_PAD), BF16)
    return pl.pallas_call(
        _gqa_kv_kernel,
        grid=(t // tm,),
        in_specs=[pl.BlockSpec((tm, k), lambda i: (i, 0)),
                  pl.BlockSpec((k, n), lambda i: (0, 0)),
                  pl.BlockSpec((1, GQA_HEAD_DIM), lambda i: (0, 0)),
                  tab_spec, tab_spec],
        out_specs=[out_spec, out_spec],
        out_shape=[out_shape, out_shape],
        compiler_params=_params("parallel"),
        name="gqa_kv_proj",
    )(a, w_kv, k_norm.reshape(1, GQA_HEAD_DIM), *tabs)


ATTN_PAD = 256
FAST_PATH_MIN_SUM = 2.0 ** -60
BOUND_MARGIN = 1.0 + 2.0 ** -6
NT_DIMS = (((1,), (1,)), ((), ()))


def _ones_col(rows, dtype):
    lane = lax.broadcasted_iota(jnp.int32, (rows, LANES), 1)
    return jnp.where(lane == 0, 1.0, 0.0).astype(dtype)


VT_ROWS = 144


def _flash_kernel(q_ref, k_ref, vt_ref, o_ref, kmax_scr, acct_scr, acc_scr, m_scr, l_scr, *,
                  n_heads, dq, dv, shared_kv, shift_col, chunk):
    seq = k_ref.shape[0]
    tq = q_ref.shape[0]
    n_chunks = seq // chunk
    n_kv = 1 if shared_kv else n_heads

    def kv_chunk(c, hk):
        r0 = c * chunk if isinstance(c, int) else pl.multiple_of(c * chunk, chunk)
        kc = k_ref[pl.ds(r0, chunk), hk * ATTN_PAD:(hk + 1) * ATTN_PAD]
        vtc = vt_ref[hk * VT_ROWS:(hk + 1) * VT_ROWS, pl.ds(r0, chunk)]
        return kc, vtc

    @pl.when(pl.program_id(2) == 0)
    def _():
        for hk in range(n_kv):
            def norm_body(c, best):
                kf = kv_chunk(c, hk)[0].astype(F32)
                row = jnp.sum(kf * kf, axis=-1, keepdims=True)
                return jnp.maximum(best, jnp.max(row, axis=0, keepdims=True))
            best = lax.fori_loop(0, n_chunks, norm_body, jnp.zeros((1, 1), F32))
            kmax_scr[hk] = jnp.broadcast_to(jnp.sqrt(best), kmax_scr.shape[1:])

    lane = lax.broadcasted_iota(jnp.int32, (tq, LANES), 1)
    ok = None
    for h in range(n_heads):
        hk = 0 if shared_kv else h
        q = q_ref[:, h * dq:(h + 1) * dq]
        qf = q.astype(F32)
        shift = jnp.sqrt(jnp.sum(qf * qf, axis=-1, keepdims=True)) * kmax_scr[hk][0:1, 0:1] * BOUND_MARGIN
        hi = qf[:, LANES:] if dq > LANES else jnp.zeros((tq, LANES), F32)
        hi = jnp.where(lane == shift_col - LANES, -shift, hi).astype(q.dtype)
        qx = jnp.concatenate([q[:, :LANES], hi], axis=1)
        for c in range(n_chunks):
            kc, vtc = kv_chunk(c, hk)
            st = lax.dot_general(kc, qx, NT_DIMS, preferred_element_type=F32)
            part = jnp.dot(vtc, jnp.exp2(st).astype(vtc.dtype), preferred_element_type=F32)
            if c == 0:
                acct_scr[...] = part
            else:
                acct_scr[...] += part
        acct = acct_scr[...]
        l = acct[dv:dv + 1, :]
        o_ref[:, h * dv:(h + 1) * dv] = (acct[:dv, :] / l).T.astype(o_ref.dtype)
        head_ok = jnp.min(l) >= FAST_PATH_MIN_SUM
        ok = head_ok if ok is None else jnp.logical_and(ok, head_ok)

    @pl.when(jnp.logical_not(ok))
    def _():
        for h in range(n_heads):
            hk = 0 if shared_kv else h
            q = q_ref[:, h * dq:(h + 1) * dq]
            m_scr[...] = jnp.full(m_scr.shape, -jnp.inf, F32)
            l_scr[...] = jnp.zeros(l_scr.shape, F32)
            acc_scr[...] = jnp.zeros(acc_scr.shape, F32)

            def exact_body(c, carry):
                kc, vtc = kv_chunk(c, hk)
                s = lax.dot_general(q, kc[:, :dq], NT_DIMS, preferred_element_type=F32)
                m_prev = m_scr[:, 0:1]
                m_new = jnp.maximum(m_prev, jnp.max(s, axis=-1, keepdims=True))
                alpha = jnp.exp2(m_prev - m_new)
                p = jnp.exp2(s - m_new)
                l_new = alpha * l_scr[:, 0:1] + jnp.sum(p, axis=-1, keepdims=True)
                pv = lax.dot_general(p.astype(vtc.dtype), vtc[:dv, :], NT_DIMS, preferred_element_type=F32)
                acc_scr[...] = alpha * acc_scr[...] + pv
                m_scr[...] = jnp.broadcast_to(m_new, m_scr.shape)
                l_scr[...] = jnp.broadcast_to(l_new, l_scr.shape)
                return carry

            lax.fori_loop(0, n_chunks, exact_body, 0)
            o_ref[:, h * dv:(h + 1) * dv] = (acc_scr[...] / l_scr[:, 0:1]).astype(o_ref.dtype)


def _flash(q, k, vt, seq, n_groups, n_heads, dq, shared_kv, shift_col, tq_target=512, chunk_target=1024):
    t = q.shape[0]
    n_seq = t // seq
    dv = ATTN_PAD // 2
    tq = _pick(seq, tq_target, LANES)
    chunk = _pick(seq, chunk_target, LANES)
    qb = seq // tq
    n_kv = 1 if shared_kv else n_heads
    return pl.pallas_call(
        functools.partial(_flash_kernel, n_heads=n_heads, dq=dq, dv=dv, shared_kv=shared_kv,
                          shift_col=shift_col, chunk=chunk),
        grid=(n_seq, n_groups, qb),
        in_specs=[pl.BlockSpec((tq, n_heads * dq), lambda b, g, i: (b * qb + i, g)),
                  pl.BlockSpec((seq, n_kv * ATTN_PAD), lambda b, g, i: (b, g)),
                  pl.BlockSpec((n_kv * VT_ROWS, seq), lambda b, g, i: (g, b))],
        out_specs=pl.BlockSpec((tq, n_heads * dv), lambda b, g, i: (b * qb + i, g)),
        out_shape=jax.ShapeDtypeStruct((t, n_groups * n_heads * dv), BF16),
        scratch_shapes=[pltpu.VMEM((n_kv, SUBLANES, LANES), F32),
                        pltpu.VMEM((VT_ROWS, tq), F32),
                        pltpu.VMEM((tq, dv), F32),
                        pltpu.VMEM((tq, LANES), F32),
                        pltpu.VMEM((tq, LANES), F32)],
        compiler_params=_params("parallel", "parallel", "arbitrary"),
        name="flash_attention",
    )(q, k, vt)


def _transpose_values(v, n_heads):
    t = v.shape[0]
    vt = v.reshape(t, n_heads, ATTN_PAD)[:, :, :VT_ROWS]
    return vt.transpose(1, 2, 0).reshape(n_heads * VT_ROWS, t)


POOL_HALO = 16


def _pool_kernel(prev_ref, cur_ref, next_ref, w_ref, s_ref, o_ref, ext_ref, *, seq, group):
    tm = cur_ref.shape[0]
    ext_ref[0:POOL_HALO, :] = prev_ref[...]
    ext_ref[POOL_HALO:POOL_HALO + tm, :] = cur_ref[...]
    ext_ref[POOL_HALO + tm:, :] = next_ref[...]
    pos = (pl.program_id(0) * tm) % seq + lax.broadcasted_iota(jnp.int32, (tm, 1), 0)
    for gi, win in enumerate(POOL_WINDOWS):
        cols = slice(gi * group, (gi + 1) * group)
        total = jnp.zeros((tm, group), F32)
        count = jnp.zeros((tm, 1), F32)
        for off in range(-(win // 2), win - win // 2):
            valid = jnp.logical_and(pos + off >= 0, pos + off < seq)
            term = ext_ref[POOL_HALO + off:POOL_HALO + off + tm, cols]
            total = total + jnp.where(valid, term, 0.0)
            count = count + valid.astype(F32)
        diff = (total / count - cur_ref[:, cols]).astype(BF16)
        y = jnp.dot(diff, w_ref[gi], preferred_element_type=F32) * s_ref[:, cols]
        o_ref[:, cols] = y.astype(o_ref.dtype)


def _pool(z, pool_w, pool_scale, seq):
    t = z.shape[0]
    n_groups, group, _ = pool_w.shape
    width = n_groups * group
    tm = _pick(seq, 256, POOL_HALO)
    halo_blocks = tm // POOL_HALO
    last_halo = t // POOL_HALO - 1
    return pl.pallas_call(
        functools.partial(_pool_kernel, seq=seq, group=group),
        grid=(t // tm,),
        in_specs=[
            pl.BlockSpec((POOL_HALO, width), lambda i: (jnp.maximum(i * halo_blocks - 1, 0), 0)),
            pl.BlockSpec((tm, width), lambda i: (i, 0)),
            pl.BlockSpec((POOL_HALO, width), lambda i: (jnp.minimum((i + 1) * halo_blocks, last_halo), 0)),
            pl.BlockSpec((n_groups, group, group), lambda i: (0, 0, 0)),
            pl.BlockSpec((1, width), lambda i: (0, 0)),
        ],
        out_specs=pl.BlockSpec((tm, width), lambda i: (i, 0)),
        out_shape=jax.ShapeDtypeStruct((t, width), BF16),
        scratch_shapes=[pltpu.VMEM((tm + 2 * POOL_HALO, width), F32)],
        compiler_params=_params("parallel"),
        name="pool_mixer",
    )(z, z, z, pool_w, pool_scale.reshape(1, width))


FFT_INNER = 128


def _fnet_chan_kernel(u_ref, m_ref, o_ref):
    gw = u_ref.shape[1]
    res = jnp.dot(u_ref[...], m_ref[...], precision=HIGHEST, preferred_element_type=F32)
    o_ref[0, 0] = res[:, :gw]
    o_ref[0, 1] = res[:, gw:]


def _fnet_outer_kernel(v_ref, m_ref, o_ref):
    two, n1, rows, width = v_ref.shape[1:]
    m = m_ref[...]
    for r in range(rows):
        x = v_ref[0, :, :, r, :].reshape(two * n1, width)
        y = jnp.dot(m, x, precision=HIGHEST, preferred_element_type=F32)
        o_ref[0, :, :, r, :] = y.reshape(two, n1, width)


def _fnet_inner_kernel(a_ref, g_ref, o_ref, *, norm):
    two, group, inner, width = a_ref.shape[1:]
    for r in range(group):
        x = a_ref[0, :, r].reshape(two * inner, width)
        y = jnp.dot(g_ref[r], x, precision=HIGHEST, preferred_element_type=F32)
        o_ref[0, :, r, :] = (y * norm).astype(o_ref.dtype)


def _fnet_tables(seq, gw):
    n1 = seq // FFT_INNER
    ch = jnp.arange(gw, dtype=jnp.int32)
    ang = (2.0 * math.pi / gw) * ((ch[:, None] * ch[None, :]) % gw).astype(F32)
    chan = jnp.concatenate([jnp.cos(ang), -jnp.sin(ang)], axis=1)
    i1 = jnp.arange(n1, dtype=jnp.int32)
    ang = (2.0 * math.pi / n1) * ((i1[:, None] * i1[None, :]) % n1).astype(F32)
    c, s = jnp.cos(ang), jnp.sin(ang)
    outer = jnp.concatenate([jnp.concatenate([c, s], axis=1),
                             jnp.concatenate([-s, c], axis=1)], axis=0)
    k2 = jnp.arange(FFT_INNER, dtype=jnp.int32)
    kk = i1[:, None, None] + n1 * k2[None, :, None]
    ang = (2.0 * math.pi / seq) * ((kk * k2[None, None, :]) % seq).astype(F32)
    inner = jnp.concatenate([jnp.cos(ang), jnp.sin(ang)], axis=2)
    return chan, outer, inner


def _fourier(u, seq):
    t, width = u.shape
    n_seq = t // seq
    gw = width // FNET_GROUPS
    n1 = seq // FFT_INNER
    chan, outer, inner = _fnet_tables(seq, gw)
    tm = _pick(seq, 512, SUBLANES)
    per_seq = seq // tm
    v = pl.pallas_call(
        _fnet_chan_kernel,
        grid=(t // tm, FNET_GROUPS),
        in_specs=[pl.BlockSpec((tm, gw), lambda i, g: (i, g)),
                  pl.BlockSpec((gw, 2 * gw), lambda i, g: (0, 0))],
        out_specs=pl.BlockSpec((1, 2, tm, gw), lambda i, g: (i // per_seq, 0, i % per_seq, g)),
        out_shape=jax.ShapeDtypeStruct((n_seq, 2, seq, width), F32),
        compiler_params=_params("parallel", "parallel"),
        name="fnet_channel_dft",
    )(u, chan)
    outer_spec = pl.BlockSpec((1, 2, n1, SUBLANES, width), lambda b, j: (b, 0, 0, j, 0))
    a = pl.pallas_call(
        _fnet_outer_kernel,
        grid=(n_seq, FFT_INNER // SUBLANES),
        in_specs=[outer_spec, pl.BlockSpec((2 * n1, 2 * n1), lambda b, j: (0, 0))],
        out_specs=outer_spec,
        out_shape=jax.ShapeDtypeStruct((n_seq, 2, n1, FFT_INNER, width), F32),
        compiler_params=_params("parallel", "parallel"),
        name="fnet_outer_dft",
    )(v.reshape(n_seq, 2, n1, FFT_INNER, width), outer)
    group = min(SUBLANES, n1)
    f = pl.pallas_call(
        functools.partial(_fnet_inner_kernel, norm=1.0 / math.sqrt(seq * gw)),
        grid=(n_seq, n1 // group),
        in_specs=[pl.BlockSpec((1, 2, group, FFT_INNER, width), lambda b, k: (b, 0, k, 0, 0)),
                  pl.BlockSpec((group, FFT_INNER, 2 * FFT_INNER), lambda b, k: (k, 0, 0))],
        out_specs=pl.BlockSpec((1, FFT_INNER, group, width), lambda b, k: (b, 0, k, 0)),
        out_shape=jax.ShapeDtypeStruct((n_seq, FFT_INNER, n1, width), F32),
        compiler_params=_params("parallel", "parallel"),
        name="fnet_inner_dft",
    )(a, inner)
    return f.reshape(t, width)


def _rope_freqs(dim):
    return ROPE_THETA ** (-jnp.arange(0, dim, 2, dtype=F32) / dim)


def _pair_tables(ang_a, ang_b):
    ca, sa = jnp.cos(ang_a), jnp.sin(ang_a)
    zero = jnp.zeros_like(ca)
    if ang_b is None:
        cb, sb = zero, zero
    else:
        cb, sb = jnp.cos(ang_b), jnp.sin(ang_b)
    cos = jnp.concatenate([ca, cb, ca, cb], axis=1)
    sin = jnp.concatenate([-sa, -sb, sa, sb], axis=1)
    return cos, sin


def _pad_cols(w, width):
    return jnp.pad(w, ((0, 0), (0, width - w.shape[1])))


def _even_layer(x, mods, seq, norm_w, w_in, pool_w, pool_scale, q_norm, wq_b, kv_norm, wkv_b, w_out, tabs):
    pool_width = pool_scale.shape[0]
    q_rank = q_norm.shape[0]
    kv_rank = kv_norm.shape[0]
    heads = wq_b.shape[1] // (MLA_NOPE + MLA_ROPE)
    half = MLA_ROPE // 2
    kr_lo = pool_width + q_rank + kv_rank
    w_in_p = jnp.concatenate([w_in[:, :kr_lo], _pad_cols(w_in[:, kr_lo:kr_lo + half], ROPE_SPLIT),
                              _pad_cols(w_in[:, kr_lo + half:], ROPE_SPLIT)], axis=1)
    w_in_p = _pad_cols(w_in_p, -(-w_in_p.shape[1] // MXU_WIDTH) * MXU_WIDTH).astype(BF16)
    wq = wq_b.reshape(q_rank, heads, MLA_NOPE + MLA_ROPE)
    spare = jnp.zeros((q_rank, heads, ROPE_SPLIT - half), wq.dtype)
    wq_pad = jnp.concatenate([wq[..., :MLA_NOPE], wq[..., MLA_NOPE:MLA_NOPE + half], spare,
                              wq[..., MLA_NOPE + half:], spare], axis=-1)
    wq_pad = wq_pad.reshape(q_rank, heads * MLA_QK_PAD).astype(BF16)

    h = _normmod(x, norm_w, mods, seq, shift_row=0, scale_row=1)
    z = _matmul(h, w_in_p, F32, tm_target=1024, tn_target=768)
    a = _pool(z, pool_w.astype(BF16), pool_scale, seq)
    scale = (MLA_NOPE + MLA_ROPE) ** -0.5 * LOG2E
    q = _mla_q(z, pool_width // q_rank, q_norm, wq_pad, tabs, seq, scale)
    k, v = _mla_kv(z, (pool_width + q_rank) // kv_rank, (pool_width + q_rank + kv_rank) // LANES,
                   kv_norm, wkv_b.astype(BF16), tabs, seq)
    attn = _flash(q, k, _transpose_values(v, heads), seq, n_groups=heads // 2, n_heads=2, dq=MLA_QK_PAD, shared_kv=False,
                  shift_col=MLA_NOPE + MLA_SHIFT_LANE)
    w_out = w_out.astype(BF16)
    return _matmul_resid([a, attn], [w_out[:pool_width], w_out[pool_width:]], x, mods, seq,
                         gate_row=2, tm_target=1024, tn_target=512)


def _odd_layer(x, mods, seq, norm_w, w_in, q_norm, k_norm, fnet_w, w_out, tabs):
    fnet_width = fnet_w.shape[0]
    qkv_width = w_in.shape[1] - fnet_width
    kd = qkv_width // (GQA_GROUP + 2)
    qd = GQA_GROUP * kd
    w_in = w_in.astype(BF16)
    perm = _half_split_perm(GQA_HEAD_DIM)
    d_in = w_in.shape[0]
    w_q = w_in[:, :qd].reshape(d_in, qd // GQA_HEAD_DIM, GQA_HEAD_DIM)[:, :, perm].reshape(d_in, qd)
    w_k = w_in[:, qd:qd + kd].reshape(d_in, kd // GQA_HEAD_DIM, GQA_HEAD_DIM)[:, :, perm].reshape(d_in, kd)
    w_kv = jnp.concatenate([w_k, w_in[:, qd + kd:qd + 2 * kd]], axis=1)

    h = _normmod(x, norm_w, mods, seq, shift_row=0, scale_row=1)
    q = _gqa_qk(h, w_q, q_norm[perm], tabs, seq, GQA_HEAD_DIM ** -0.5 * LOG2E)
    k, v = _gqa_kv(h, w_kv, k_norm[perm], tabs, seq)
    u = _matmul(h, w_in[:, qd + 2 * kd:], F32)
    attn = _flash(q, k, _transpose_values(v, kd // GQA_HEAD_DIM), seq, n_groups=kd // GQA_HEAD_DIM, n_heads=GQA_GROUP,
                  dq=GQA_HEAD_DIM, shared_kv=True, shift_col=GQA_HEAD_DIM)
    f = _fourier(u, seq)
    ff = _matmul(f, fnet_w.astype(BF16), BF16)
    w_out = w_out.astype(BF16)
    return _matmul_resid([attn, ff], [w_out[:qd], w_out[qd:]], x, mods, seq,
                         gate_row=2, tm_target=1024, tn_target=512)


def _ffn(x, mods, seq, norm_w, w_gate, w_up, w_down):
    h = _normmod(x, norm_w, mods, seq, shift_row=3, scale_row=4)
    hid = _matmul_swiglu(h, w_gate.astype(BF16), w_up.astype(BF16))
    return _matmul_resid([hid], [w_down.astype(BF16)], x, mods, seq,
                         gate_row=5, tm_target=512, tn_target=512)


def kernel(x_prompt, x_sample, c_prompt, c_sample, mod_w, mod_b, norm_mix, norm_ffn, e_w_in, e_pool_w, e_pool_scale, e_q_norm, e_wq_b, e_kv_norm, e_wkv_b, e_w_out, o_w_in, o_q_norm, o_k_norm, o_fnet_w, o_w_out, ffn_gate, ffn_up, ffn_down, final_norm):
    n_prompt, seq, d = x_prompt.shape
    n_sample = x_sample.shape[0]
    assert x_sample.shape[1] == seq and seq % GRID_W == 0 and seq % FFT_INNER == 0
    n_seq = n_prompt + n_sample
    depth = mod_w.shape[0]

    x = jnp.concatenate([x_prompt.reshape(n_prompt * seq, d), x_sample.reshape(n_sample * seq, d)], axis=0)
    c = jnp.concatenate([c_prompt, c_sample], axis=0)
    c_pad = jnp.pad(c, ((0, -n_seq % SUBLANES), (0, 0)))
    mods_all = _modulation(c_pad, mod_w, mod_b).reshape(depth, c_pad.shape[0], 6, d)

    pos = jnp.arange(seq)
    ang_1d = pos.astype(F32)[:, None] * _rope_freqs(MLA_ROPE)[None, :]
    ang_row = (pos // GRID_W).astype(F32)[:, None] * _rope_freqs(GQA_HEAD_DIM // 2)[None, :]
    ang_col = (pos % GRID_W).astype(F32)[:, None] * _rope_freqs(GQA_HEAD_DIM // 2)[None, :]
    mla_tabs = _pair_tables(ang_1d, None)
    gqa_tabs = _pair_tables(ang_row, ang_col)

    for l in range(depth):
        mods = mods_all[l]
        i = l // 2
        if l % 2 == 0:
            x = _even_layer(x, mods, seq, norm_mix[l], e_w_in[i], e_pool_w[i], e_pool_scale[i], e_q_norm[i],
                            e_wq_b[i], e_kv_norm[i], e_wkv_b[i], e_w_out[i], mla_tabs)
        else:
            x = _odd_layer(x, mods, seq, norm_mix[l], o_w_in[i], o_q_norm[i], o_k_norm[i], o_fnet_w[i],
                           o_w_out[i], gqa_tabs)
        x = _ffn(x, mods, seq, norm_ffn[l], ffn_gate[l], ffn_up[l], ffn_down[l])

    y_prompt = _final_norm(x, final_norm, seq, 0, n_prompt).reshape(n_prompt, seq, d)
    y_sample = _final_norm(x, final_norm, seq, n_prompt, n_sample).reshape(n_sample, seq, d)
    return (y_prompt, y_sample)
```

```python
import functools
import math

import jax
import jax.numpy as jnp
from jax import lax
from jax.experimental import pallas as pl
from jax.experimental.pallas import tpu as pltpu

F32 = jnp.float32
BF16 = jnp.bfloat16

EPS = 1e-6
ROPE_THETA = 10000.0
GRID_W = 64
POOL_WINDOWS = (2, 4, 8, 16)
MLA_NOPE = 128
MLA_ROPE = 64
MLA_V = 128
MLA_QK_PAD = 256
MLA_SHIFT_LANE = 32
GQA_HEAD_DIM = 128
GQA_GROUP = 4
FNET_GROUPS = 4
LOG2E = 1.4426950408889634

LANES = 128
SUBLANES = 8
VMEM_LIMIT_BYTES = 52 * 1024 * 1024

HIGHEST = lax.Precision.HIGHEST


def _pick(n, target, quantum=LANES):
    best = None
    t = quantum
    while t <= min(n, target):
        if n % t == 0:
            best = t
        t += quantum
    if best is None:
        return n
    return best


def _params(*sem, vmem_limit=VMEM_LIMIT_BYTES):
    return pltpu.CompilerParams(dimension_semantics=sem, vmem_limit_bytes=vmem_limit)


ROPE_SPLIT = LANES // 2


def _rot_pair(x, c, s):
    return x * c + pltpu.roll(x, ROPE_SPLIT, 1) * s


def _half_split_perm(width):
    q = width // 4
    idx = jnp.arange(width).reshape(2, 2, q)
    return idx.transpose(1, 0, 2).reshape(width)


def _mod_kernel(c_ref, w_ref, b_ref, o_ref):
    c = c_ref[...]
    act = (c / (1.0 + jnp.exp(-c))).astype(BF16)
    w = w_ref[0].astype(BF16)
    o_ref[0] = jnp.dot(act, w, preferred_element_type=F32) + b_ref[0]


def _modulation(c_pad, mod_w, mod_b):
    depth, d, n = mod_w.shape
    rows = c_pad.shape[0]
    tn = _pick(n, 512)
    return pl.pallas_call(
        _mod_kernel,
        grid=(depth, n // tn),
        in_specs=[
            pl.BlockSpec((rows, d), lambda l, j: (0, 0)),
            pl.BlockSpec((1, d, tn), lambda l, j: (l, 0, j)),
            pl.BlockSpec((1, 1, tn), lambda l, j: (l, 0, j)),
        ],
        out_specs=pl.BlockSpec((1, rows, tn), lambda l, j: (l, 0, j)),
        out_shape=jax.ShapeDtypeStruct((depth, rows, n), F32),
        compiler_params=_params("parallel", "parallel"),
        name="modulation",
    )(c_pad, mod_w, mod_b.reshape(depth, 1, n))


def _normmod_kernel(x_ref, w_ref, m_ref, o_ref, *, shift_row, scale_row):
    x = x_ref[...]
    y = x * lax.rsqrt(jnp.mean(x * x, axis=-1, keepdims=True) + EPS) * w_ref[...]
    sc = m_ref[0, scale_row:scale_row + 1, :]
    sh = m_ref[0, shift_row:shift_row + 1, :]
    o_ref[...] = (y * (1.0 + sc) + sh).astype(o_ref.dtype)


def _norm_kernel(x_ref, w_ref, o_ref):
    x = x_ref[...]
    y = x * lax.rsqrt(jnp.mean(x * x, axis=-1, keepdims=True) + EPS) * w_ref[...]
    o_ref[...] = y.astype(o_ref.dtype)


def _stack_normmod_kernel(xa_ref, xb_ref, w_ref, m_ref, x_ref, h_ref, *, n_first, shift_row, scale_row):
    def emit(src_ref):
        x = src_ref[...]
        x_ref[...] = x
        y = x * lax.rsqrt(jnp.mean(x * x, axis=-1, keepdims=True) + EPS) * w_ref[...]
        sc = m_ref[0, scale_row:scale_row + 1, :]
        sh = m_ref[0, shift_row:shift_row + 1, :]
        h_ref[...] = (y * (1.0 + sc) + sh).astype(h_ref.dtype)

    first = pl.program_id(0) < n_first
    pl.when(first)(lambda: emit(xa_ref))
    pl.when(jnp.logical_not(first))(lambda: emit(xb_ref))


def _stack_normmod(xa, xb, w, mods, seq, shift_row, scale_row):
    d = xa.shape[1]
    t = xa.shape[0] + xb.shape[0]
    tm = _pick(seq, 256, SUBLANES)
    per_seq = seq // tm
    n_first = xa.shape[0] // tm
    return pl.pallas_call(
        functools.partial(_stack_normmod_kernel, n_first=n_first, shift_row=shift_row, scale_row=scale_row),
        grid=(t // tm,),
        in_specs=[
            pl.BlockSpec((tm, d), lambda i: (jnp.minimum(i, n_first - 1), 0)),
            pl.BlockSpec((tm, d), lambda i: (jnp.maximum(i - n_first, 0), 0)),
            pl.BlockSpec((1, d), lambda i: (0, 0)),
            pl.BlockSpec((1, 6, d), lambda i: (i // per_seq, 0, 0)),
        ],
        out_specs=[pl.BlockSpec((tm, d), lambda i: (i, 0)), pl.BlockSpec((tm, d), lambda i: (i, 0))],
        out_shape=[jax.ShapeDtypeStruct((t, d), F32), jax.ShapeDtypeStruct((t, d), BF16)],
        compiler_params=_params("arbitrary"),
        name="stack_normmod",
    )(xa, xb, w.reshape(1, d), mods)


def _normmod(x, w, mods, seq, shift_row, scale_row):
    t, d = x.shape
    tm = _pick(seq, 256, SUBLANES)
    per_seq = seq // tm
    return pl.pallas_call(
        functools.partial(_normmod_kernel, shift_row=shift_row, scale_row=scale_row),
        grid=(t // tm,),
        in_specs=[
            pl.BlockSpec((tm, d), lambda i: (i, 0)),
            pl.BlockSpec((1, d), lambda i: (0, 0)),
            pl.BlockSpec((1, 6, d), lambda i: (i // per_seq, 0, 0)),
        ],
        out_specs=pl.BlockSpec((tm, d), lambda i: (i, 0)),
        out_shape=jax.ShapeDtypeStruct((t, d), BF16),
        compiler_params=_params("parallel"),
        name="normmod",
    )(x, w.reshape(1, d), mods)


def _final_norm(x, w, seq, first_seq, n_seq):
    d = x.shape[1]
    tm = _pick(seq, 256, SUBLANES)
    first_block = first_seq * (seq // tm)
    return pl.pallas_call(
        _norm_kernel,
        grid=(n_seq * seq // tm,),
        in_specs=[pl.BlockSpec((tm, d), lambda i: (first_block + i, 0)),
                  pl.BlockSpec((1, d), lambda i: (0, 0))],
        out_specs=pl.BlockSpec((tm, d), lambda i: (i, 0)),
        out_shape=jax.ShapeDtypeStruct((n_seq * seq, d), F32),
        compiler_params=_params("parallel"),
        name="final_norm",
    )(x, w.reshape(1, d))


def _mm_plain_kernel(a_ref, w_ref, o_ref):
    a = a_ref[...].astype(w_ref.dtype)
    o_ref[...] = jnp.dot(a, w_ref[...], preferred_element_type=F32).astype(o_ref.dtype)


def _matmul(a, w, out_dtype, tm_target=1024, tn_target=512):
    m, k = a.shape
    n = w.shape[1]
    tm = _pick(m, tm_target, SUBLANES)
    tn = _pick(n, tn_target)
    return pl.pallas_call(
        _mm_plain_kernel,
        grid=(m // tm, n // tn),
        in_specs=[pl.BlockSpec((tm, k), lambda i, j: (i, 0)),
                  pl.BlockSpec((k, tn), lambda i, j: (0, j))],
        out_specs=pl.BlockSpec((tm, tn), lambda i, j: (i, j)),
        out_shape=jax.ShapeDtypeStruct((m, n), out_dtype),
        compiler_params=_params("parallel", "parallel"),
        name="matmul",
    )(a, w)


def _mm_resid_kernel(*refs, n_a, gate_row):
    a_refs = refs[:n_a]
    w_refs = refs[n_a:2 * n_a]
    x_ref, m_ref, o_ref = refs[2 * n_a:]
    acc = jnp.dot(a_refs[0][...], w_refs[0][...], preferred_element_type=F32)
    for a_ref, w_ref in zip(a_refs[1:], w_refs[1:]):
        acc = acc + jnp.dot(a_ref[...], w_ref[...], preferred_element_type=F32)
    o_ref[...] = x_ref[...] + m_ref[0, gate_row:gate_row + 1, :] * acc


def _matmul_resid(a_list, w_list, x, mods, seq, gate_row, tm_target, tn_target):
    t, n = x.shape
    tm = _pick(seq, tm_target, SUBLANES)
    tn = _pick(n, tn_target)
    per_seq = seq // tm
    n_a = len(a_list)
    in_specs = [pl.BlockSpec((tm, a.shape[1]), lambda i, j: (i, 0)) for a in a_list]
    in_specs += [pl.BlockSpec((w.shape[0], tn), lambda i, j: (0, j)) for w in w_list]
    in_specs += [pl.BlockSpec((tm, tn), lambda i, j: (i, j)),
                 pl.BlockSpec((1, 6, tn), lambda i, j: (i // per_seq, 0, j))]
    return pl.pallas_call(
        functools.partial(_mm_resid_kernel, n_a=n_a, gate_row=gate_row),
        grid=(t // tm, n // tn),
        in_specs=in_specs,
        out_specs=pl.BlockSpec((tm, tn), lambda i, j: (i, j)),
        out_shape=jax.ShapeDtypeStruct((t, n), F32),
        compiler_params=_params("parallel", "parallel"),
        name="matmul_resid",
    )(*a_list, *w_list, x, mods)


def _mm_swiglu_kernel(a_ref, wg_ref, wu_ref, o_ref):
    a = a_ref[...]
    g = jnp.dot(a, wg_ref[...], preferred_element_type=F32)
    u = jnp.dot(a, wu_ref[...], preferred_element_type=F32)
    o_ref[...] = (g / (1.0 + jnp.exp(-g)) * u).astype(o_ref.dtype)


def _matmul_swiglu(a, wg, wu, tm_target=2048, tn_target=256):
    m, k = a.shape
    n = wg.shape[1]
    tm = _pick(m, tm_target, SUBLANES)
    tn = _pick(n, tn_target)
    return pl.pallas_call(
        _mm_swiglu_kernel,
        grid=(m // tm, n // tn),
        in_specs=[pl.BlockSpec((tm, k), lambda i, j: (i, 0)),
                  pl.BlockSpec((k, tn), lambda i, j: (0, j)),
                  pl.BlockSpec((k, tn), lambda i, j: (0, j))],
        out_specs=pl.BlockSpec((tm, tn), lambda i, j: (i, j)),
        out_shape=jax.ShapeDtypeStruct((m, n), BF16),
        compiler_params=_params("parallel", "parallel"),
        name="matmul_swiglu",
    )(a, wg, wu)


def _mla_q_kernel(a_ref, nw_ref, w_ref, c_ref, s_ref, o_ref, *, scale):
    a = a_ref[...]
    nrm = (a * lax.rsqrt(jnp.mean(a * a, axis=-1, keepdims=True) + EPS) * nw_ref[...]).astype(BF16)
    c, s = c_ref[...], s_ref[...]
    for h in range(o_ref.shape[1] // MLA_QK_PAD):
        lo = h * MLA_QK_PAD
        acc = jnp.dot(nrm, w_ref[:, lo:lo + MLA_QK_PAD], preferred_element_type=F32)
        o_ref[:, lo:lo + MLA_NOPE] = (acc[:, :MLA_NOPE] * scale).astype(o_ref.dtype)
        o_ref[:, lo + MLA_NOPE:lo + MLA_QK_PAD] = (_rot_pair(acc[:, MLA_NOPE:], c, s) * scale).astype(o_ref.dtype)


def _mla_q(z, col_block, q_norm, wq_pad, tabs, seq, scale):
    t = z.shape[0]
    rank, n = wq_pad.shape
    tm = _pick(seq, 512, SUBLANES)
    per_seq = seq // tm
    tab_spec = pl.BlockSpec((tm, LANES), lambda i: (i % per_seq, 0))
    return pl.pallas_call(
        functools.partial(_mla_q_kernel, scale=scale),
        grid=(t // tm,),
        in_specs=[pl.BlockSpec((tm, rank), lambda i: (i, col_block)),
                  pl.BlockSpec((1, rank), lambda i: (0, 0)),
                  pl.BlockSpec((rank, n), lambda i: (0, 0)),
                  tab_spec, tab_spec],
        out_specs=pl.BlockSpec((tm, n), lambda i: (i, 0)),
        out_shape=jax.ShapeDtypeStruct((t, n), BF16),
        compiler_params=_params("parallel"),
        name="mla_q_proj",
    )(z, q_norm.reshape(1, rank), wq_pad, *tabs)


def _mla_kv_kernel(a_ref, kr_ref, nw_ref, w_ref, c_ref, s_ref, k_ref, v_ref):
    a = a_ref[...]
    nrm = (a * lax.rsqrt(jnp.mean(a * a, axis=-1, keepdims=True) + EPS) * nw_ref[...]).astype(BF16)
    kr = _rot_pair(kr_ref[...], c_ref[...], s_ref[...])
    lane = lax.broadcasted_iota(jnp.int32, kr.shape, 1)
    kr = jnp.where(lane == MLA_SHIFT_LANE, 1.0, kr).astype(k_ref.dtype)
    ones = _ones_col(kr.shape[0], v_ref.dtype)
    width = MLA_NOPE + MLA_V
    for h in range(w_ref.shape[1] // width):
        acc = jnp.dot(nrm, w_ref[:, h * width:(h + 1) * width], preferred_element_type=F32)
        k_ref[:, h * MLA_QK_PAD:h * MLA_QK_PAD + MLA_NOPE] = acc[:, :MLA_NOPE].astype(k_ref.dtype)
        k_ref[:, h * MLA_QK_PAD + MLA_NOPE:(h + 1) * MLA_QK_PAD] = kr
        v_ref[:, h * ATTN_PAD:h * ATTN_PAD + MLA_V] = acc[:, MLA_NOPE:].astype(v_ref.dtype)
        v_ref[:, h * ATTN_PAD + MLA_V:(h + 1) * ATTN_PAD] = ones


def _mla_kv(z, kv_col_block, kr_col_block, kv_norm, wkv, tabs, seq):
    t = z.shape[0]
    rank, n = wkv.shape
    heads = n // (MLA_NOPE + MLA_V)
    tm = _pick(seq, 512, SUBLANES)
    per_seq = seq // tm
    tab_spec = pl.BlockSpec((tm, LANES), lambda i: (i % per_seq, 0))
    return pl.pallas_call(
        _mla_kv_kernel,
        grid=(t // tm,),
        in_specs=[pl.BlockSpec((tm, rank), lambda i: (i, kv_col_block)),
                  pl.BlockSpec((tm, LANES), lambda i: (i, kr_col_block)),
                  pl.BlockSpec((1, rank), lambda i: (0, 0)),
                  pl.BlockSpec((rank, n), lambda i: (0, 0)),
                  tab_spec, tab_spec],
        out_specs=[pl.BlockSpec((tm, heads * MLA_QK_PAD), lambda i: (i, 0)),
                   pl.BlockSpec((tm, heads * ATTN_PAD), lambda i: (i, 0))],
        out_shape=[jax.ShapeDtypeStruct((t, heads * MLA_QK_PAD), BF16),
                   jax.ShapeDtypeStruct((t, heads * ATTN_PAD), BF16)],
        compiler_params=_params("parallel"),
        name="mla_kv_proj",
    )(z, z, kv_norm.reshape(1, rank), wkv, *tabs)


MXU_WIDTH = 256


def _head_norm_rope(x, nw, c, s):
    y = x * lax.rsqrt(jnp.mean(x * x, axis=-1, keepdims=True) + EPS) * nw
    return _rot_pair(y, c, s)


def _gqa_qk_kernel(a_ref, w_ref, nw_ref, c_ref, s_ref, o_ref, *, scale):
    a = a_ref[...]
    nw, c, s = nw_ref[...], c_ref[...], s_ref[...]
    for p in range(o_ref.shape[1] // MXU_WIDTH):
        acc = jnp.dot(a, w_ref[:, p * MXU_WIDTH:(p + 1) * MXU_WIDTH], preferred_element_type=F32)
        for h in range(MXU_WIDTH // GQA_HEAD_DIM):
            lo = h * GQA_HEAD_DIM
            y = _head_norm_rope(acc[:, lo:lo + GQA_HEAD_DIM], nw, c, s) * scale
            o_ref[:, p * MXU_WIDTH + lo:p * MXU_WIDTH + lo + GQA_HEAD_DIM] = y.astype(o_ref.dtype)


def _gqa_qk(a, w, head_norm, tabs, seq, scale):
    t, k = a.shape
    n = w.shape[1]
    tm = _pick(seq, 1024, SUBLANES)
    tn = _pick(n, 512)
    per_seq = seq // tm
    tab_spec = pl.BlockSpec((tm, LANES), lambda i, j: (i % per_seq, 0))
    return pl.pallas_call(
        functools.partial(_gqa_qk_kernel, scale=scale),
        grid=(t // tm, n // tn),
        in_specs=[pl.BlockSpec((tm, k), lambda i, j: (i, 0)),
                  pl.BlockSpec((k, tn), lambda i, j: (0, j)),
                  pl.BlockSpec((1, GQA_HEAD_DIM), lambda i, j: (0, 0)),
                  tab_spec, tab_spec],
        out_specs=pl.BlockSpec((tm, tn), lambda i, j: (i, j)),
        out_shape=jax.ShapeDtypeStruct((t, n), BF16),
        compiler_params=_params("parallel", "parallel"),
        name="gqa_qk_proj",
    )(a, w, head_norm.reshape(1, GQA_HEAD_DIM), *tabs)


def _gqa_kv_kernel(a_ref, w_ref, nw_ref, c_ref, s_ref, k_ref, v_ref):
    a = a_ref[...]
    nw, c, s = nw_ref[...], c_ref[...], s_ref[...]
    kd = w_ref.shape[1] // 2
    ones = _ones_col(a.shape[0], k_ref.dtype)
    per_dot = MXU_WIDTH // GQA_HEAD_DIM
    for p in range(w_ref.shape[1] // MXU_WIDTH):
        acc = jnp.dot(a, w_ref[:, p * MXU_WIDTH:(p + 1) * MXU_WIDTH], preferred_element_type=F32)
        for h in range(per_dot):
            x = acc[:, h * GQA_HEAD_DIM:(h + 1) * GQA_HEAD_DIM]
            col = p * MXU_WIDTH + h * GQA_HEAD_DIM
            if col < kd:
                head = col // GQA_HEAD_DIM
                k_ref[:, head * ATTN_PAD:head * ATTN_PAD + GQA_HEAD_DIM] = (
                    _head_norm_rope(x, nw, c, s).astype(k_ref.dtype))
                k_ref[:, head * ATTN_PAD + GQA_HEAD_DIM:(head + 1) * ATTN_PAD] = ones
            else:
                head = (col - kd) // GQA_HEAD_DIM
                v_ref[:, head * ATTN_PAD:head * ATTN_PAD + GQA_HEAD_DIM] = x.astype(v_ref.dtype)
                v_ref[:, head * ATTN_PAD + GQA_HEAD_DIM:(head + 1) * ATTN_PAD] = ones


def _gqa_kv(a, w_kv, k_norm, tabs, seq):
    t, k = a.shape
    n = w_kv.shape[1]
    heads = n // (2 * GQA_HEAD_DIM)
    tm = _pick(seq, 512, SUBLANES)
    per_seq = seq // tm
    tab_spec = pl.BlockSpec((tm, LANES), lambda i: (i % per_seq, 0))
    out_spec = pl.BlockSpec((tm, heads * ATTN_PAD), lambda i: (i, 0))
    out_shape = jax.ShapeDtypeStruct((t, heads * ATTN_PAD), BF16)
    return pl.pallas_call(
        _gqa_kv_kernel,
        grid=(t // tm,),
        in_specs=[pl.BlockSpec((tm, k), lambda i: (i, 0)),
                  pl.BlockSpec((k, n), lambda i: (0, 0)),
                  pl.BlockSpec((1, GQA_HEAD_DIM), lambda i: (0, 0)),
                  tab_spec, tab_spec],
        out_specs=[out_spec, out_spec],
        out_shape=[out_shape, out_shape],
        compiler_params=_params("parallel"),
        name="gqa_kv_proj",
    )(a, w_kv, k_norm.reshape(1, GQA_HEAD_DIM), *tabs)


ATTN_PAD = 256
FAST_PATH_MIN_SUM = 2.0 ** -60
BOUND_MARGIN = 1.0 + 2.0 ** -6
NT_DIMS = (((1,), (1,)), ((), ()))


def _ones_col(rows, dtype):
    lane = lax.broadcasted_iota(jnp.int32, (rows, LANES), 1)
    return jnp.where(lane == 0, 1.0, 0.0).astype(dtype)


VT_ROWS = 144


def _flash_kernel(q_ref, k_ref, vt_ref, o_ref, kmax_scr, acct_scr, acc_scr, m_scr, l_scr, *,
                  n_heads, dq, dv, shared_kv, shift_col, chunk):
    seq = k_ref.shape[0]
    tq = q_ref.shape[0]
    n_chunks = seq // chunk
    n_kv = 1 if shared_kv else n_heads

    def kv_chunk(c, hk):
        r0 = c * chunk if isinstance(c, int) else pl.multiple_of(c * chunk, chunk)
        kc = k_ref[pl.ds(r0, chunk), hk * ATTN_PAD:(hk + 1) * ATTN_PAD]
        vtc = vt_ref[hk * VT_ROWS:(hk + 1) * VT_ROWS, pl.ds(r0, chunk)]
        return kc, vtc

    @pl.when(pl.program_id(2) == 0)
    def _():
        for hk in range(n_kv):
            def norm_body(c, best):
                kf = kv_chunk(c, hk)[0].astype(F32)
                row = jnp.sum(kf * kf, axis=-1, keepdims=True)
                return jnp.maximum(best, jnp.max(row, axis=0, keepdims=True))
            best = lax.fori_loop(0, n_chunks, norm_body, jnp.zeros((1, 1), F32))
            kmax_scr[hk] = jnp.broadcast_to(jnp.sqrt(best), kmax_scr.shape[1:])

    lane = lax.broadcasted_iota(jnp.int32, (tq, LANES), 1)
    ok = None
    for h in range(n_heads):
        hk = 0 if shared_kv else h
        q = q_ref[:, h * dq:(h + 1) * dq]
        qf = q.astype(F32)
        shift = jnp.sqrt(jnp.sum(qf * qf, axis=-1, keepdims=True)) * kmax_scr[hk][0:1, 0:1] * BOUND_MARGIN
        hi = qf[:, LANES:] if dq > LANES else jnp.zeros((tq, LANES), F32)
        hi = jnp.where(lane == shift_col - LANES, -shift, hi).astype(q.dtype)
        qx = jnp.concatenate([q[:, :LANES], hi], axis=1)
        for c in range(n_chunks):
            kc, vtc = kv_chunk(c, hk)
            st = lax.dot_general(kc, qx, NT_DIMS, preferred_element_type=F32)
            part = jnp.dot(vtc, jnp.exp2(st).astype(vtc.dtype), preferred_element_type=F32)
            if c == 0:
                acct_scr[...] = part
            else:
                acct_scr[...] += part
        acct = acct_scr[...]
        l = acct[dv:dv + 1, :]
        o_ref[:, h * dv:(h + 1) * dv] = (acct[:dv, :] / l).T.astype(o_ref.dtype)
        head_ok = jnp.min(l) >= FAST_PATH_MIN_SUM
        ok = head_ok if ok is None else jnp.logical_and(ok, head_ok)

    @pl.when(jnp.logical_not(ok))
    def _():
        for h in range(n_heads):
            hk = 0 if shared_kv else h
            q = q_ref[:, h * dq:(h + 1) * dq]
            m_scr[...] = jnp.full(m_scr.shape, -jnp.inf, F32)
            l_scr[...] = jnp.zeros(l_scr.shape, F32)
            acc_scr[...] = jnp.zeros(acc_scr.shape, F32)

            def exact_body(c, carry):
                kc, vtc = kv_chunk(c, hk)
                s = lax.dot_general(q, kc[:, :dq], NT_DIMS, preferred_element_type=F32)
                m_prev = m_scr[:, 0:1]
                m_new = jnp.maximum(m_prev, jnp.max(s, axis=-1, keepdims=True))
                alpha = jnp.exp2(m_prev - m_new)
                p = jnp.exp2(s - m_new)
                l_new = alpha * l_scr[:, 0:1] + jnp.sum(p, axis=-1, keepdims=True)
                pv = lax.dot_general(p.astype(vtc.dtype), vtc[:dv, :], NT_DIMS, preferred_element_type=F32)
                acc_scr[...] = alpha * acc_scr[...] + pv
                m_scr[...] = jnp.broadcast_to(m_new, m_scr.shape)
                l_scr[...] = jnp.broadcast_to(l_new, l_scr.shape)
                return carry

            lax.fori_loop(0, n_chunks, exact_body, 0)
            o_ref[:, h * dv:(h + 1) * dv] = (acc_scr[...] / l_scr[:, 0:1]).astype(o_ref.dtype)


def _flash(q, k, vt, seq, n_groups, n_heads, dq, shared_kv, shift_col, tq_target=512, chunk_target=1024):
    t = q.shape[0]
    n_seq = t // seq
    dv = ATTN_PAD // 2
    tq = _pick(seq, tq_target, LANES)
    chunk = _pick(seq, chunk_target, LANES)
    qb = seq // tq
    n_kv = 1 if shared_kv else n_heads
    return pl.pallas_call(
        functools.partial(_flash_kernel, n_heads=n_heads, dq=dq, dv=dv, shared_kv=shared_kv,
                          shift_col=shift_col, chunk=chunk),
        grid=(n_seq, n_groups, qb),
        in_specs=[pl.BlockSpec((tq, n_heads * dq), lambda b, g, i: (b * qb + i, g)),
                  pl.BlockSpec((seq, n_kv * ATTN_PAD), lambda b, g, i: (b, g)),
                  pl.BlockSpec((n_kv * VT_ROWS, seq), lambda b, g, i: (g, b))],
        out_specs=pl.BlockSpec((tq, n_heads * dv), lambda b, g, i: (b * qb + i, g)),
        out_shape=jax.ShapeDtypeStruct((t, n_groups * n_heads * dv), BF16),
        scratch_shapes=[pltpu.VMEM((n_kv, SUBLANES, LANES), F32),
                        pltpu.VMEM((VT_ROWS, tq), F32),
                        pltpu.VMEM((tq, dv), F32),
                        pltpu.VMEM((tq, LANES), F32),
                        pltpu.VMEM((tq, LANES), F32)],
        compiler_params=_params("parallel", "parallel", "arbitrary"),
        name="flash_attention",
    )(q, k, vt)


def _transpose_values(v, n_heads):
    t = v.shape[0]
    vt = v.reshape(t, n_heads, ATTN_PAD)[:, :, :VT_ROWS]
    return vt.transpose(1, 2, 0).reshape(n_heads * VT_ROWS, t)


POOL_HALO = 16


def _pool_kernel(prev_ref, cur_ref, next_ref, w_ref, s_ref, o_ref, ext_ref, *, seq, group):
    tm = cur_ref.shape[0]
    ext_ref[0:POOL_HALO, :] = prev_ref[...]
    ext_ref[POOL_HALO:POOL_HALO + tm, :] = cur_ref[...]
    ext_ref[POOL_HALO + tm:, :] = next_ref[...]
    pos = (pl.program_id(0) * tm) % seq + lax.broadcasted_iota(jnp.int32, (tm, 1), 0)
    for gi, win in enumerate(POOL_WINDOWS):
        cols = slice(gi * group, (gi + 1) * group)
        total = jnp.zeros((tm, group), F32)
        count = jnp.zeros((tm, 1), F32)
        for off in range(-(win // 2), win - win // 2):
            valid = jnp.logical_and(pos + off >= 0, pos + off < seq)
            term = ext_ref[POOL_HALO + off:POOL_HALO + off + tm, cols]
            total = total + jnp.where(valid, term, 0.0)
            count = count + valid.astype(F32)
        diff = (total / count - cur_ref[:, cols]).astype(BF16)
        y = jnp.dot(diff, w_ref[gi], preferred_element_type=F32) * s_ref[:, cols]
        o_ref[:, cols] = y.astype(o_ref.dtype)


def _pool(z, pool_w, pool_scale, seq):
    t = z.shape[0]
    n_groups, group, _ = pool_w.shape
    width = n_groups * group
    tm = _pick(seq, 256, POOL_HALO)
    halo_blocks = tm // POOL_HALO
    last_halo = t // POOL_HALO - 1
    return pl.pallas_call(
        functools.partial(_pool_kernel, seq=seq, group=group),
        grid=(t // tm,),
        in_specs=[
            pl.BlockSpec((POOL_HALO, width), lambda i: (jnp.maximum(i * halo_blocks - 1, 0), 0)),
            pl.BlockSpec((tm, width), lambda i: (i, 0)),
            pl.BlockSpec((POOL_HALO, width), lambda i: (jnp.minimum((i + 1) * halo_blocks, last_halo), 0)),
            pl.BlockSpec((n_groups, group, group), lambda i: (0, 0, 0)),
            pl.BlockSpec((1, width), lambda i: (0, 0)),
        ],
        out_specs=pl.BlockSpec((tm, width), lambda i: (i, 0)),
        out_shape=jax.ShapeDtypeStruct((t, width), BF16),
        scratch_shapes=[pltpu.VMEM((tm + 2 * POOL_HALO, width), F32)],
        compiler_params=_params("parallel"),
        name="pool_mixer",
    )(z, z, z, pool_w, pool_scale.reshape(1, width))


FFT_INNER = 128


def _fnet_chan_kernel(u_ref, m_ref, o_ref):
    gw = u_ref.shape[1]
    res = jnp.dot(u_ref[...], m_ref[...], precision=HIGHEST, preferred_element_type=F32)
    o_ref[0, 0] = res[:, :gw]
    o_ref[0, 1] = res[:, gw:]


def _fnet_outer_kernel(v_ref, m_ref, o_ref):
    two, n1, rows, width = v_ref.shape[1:]
    m = m_ref[...]
    for r in range(rows):
        x = v_ref[0, :, :, r, :].reshape(two * n1, width)
        y = jnp.dot(m, x, precision=HIGHEST, preferred_element_type=F32)
        o_ref[0, :, :, r, :] = y.reshape(two, n1, width)


def _fnet_inner_kernel(a_ref, g_ref, o_ref, *, norm):
    two, group, inner, width = a_ref.shape[1:]
    for r in range(group):
        x = a_ref[0, :, r].reshape(two * inner, width)
        y = jnp.dot(g_ref[r], x, precision=HIGHEST, preferred_element_type=F32)
        o_ref[0, :, r, :] = (y * norm).astype(o_ref.dtype)


def _fnet_tables(seq, gw):
    n1 = seq // FFT_INNER
    ch = jnp.arange(gw, dtype=jnp.int32)
    ang = (2.0 * math.pi / gw) * ((ch[:, None] * ch[None, :]) % gw).astype(F32)
    chan = jnp.concatenate([jnp.cos(ang), -jnp.sin(ang)], axis=1)
    i1 = jnp.arange(n1, dtype=jnp.int32)
    ang = (2.0 * math.pi / n1) * ((i1[:, None] * i1[None, :]) % n1).astype(F32)
    c, s = jnp.cos(ang), jnp.sin(ang)
    outer = jnp.concatenate([jnp.concatenate([c, s], axis=1),
                             jnp.concatenate([-s, c], axis=1)], axis=0)
    k2 = jnp.arange(FFT_INNER, dtype=jnp.int32)
    kk = i1[:, None, None] + n1 * k2[None, :, None]
    ang = (2.0 * math.pi / seq) * ((kk * k2[None, None, :]) % seq).astype(F32)
    inner = jnp.concatenate([jnp.cos(ang), jnp.sin(ang)], axis=2)
    return chan, outer, inner


def _fourier(u, seq):
    t, width = u.shape
    n_seq = t // seq
    gw = width // FNET_GROUPS
    n1 = seq // FFT_INNER
    chan, outer, inner = _fnet_tables(seq, gw)
    tm = _pick(seq, 512, SUBLANES)
    per_seq = seq // tm
    v = pl.pallas_call(
        _fnet_chan_kernel,
        grid=(t // tm, FNET_GROUPS),
        in_specs=[pl.BlockSpec((tm, gw), lambda i, g: (i, g)),
                  pl.BlockSpec((gw, 2 * gw), lambda i, g: (0, 0))],
        out_specs=pl.BlockSpec((1, 2, tm, gw), lambda i, g: (i // per_seq, 0, i % per_seq, g)),
        out_shape=jax.ShapeDtypeStruct((n_seq, 2, seq, width), F32),
        compiler_params=_params("parallel", "parallel"),
        name="fnet_channel_dft",
    )(u, chan)
    outer_spec = pl.BlockSpec((1, 2, n1, SUBLANES, width), lambda b, j: (b, 0, 0, j, 0))
    a = pl.pallas_call(
        _fnet_outer_kernel,
        grid=(n_seq, FFT_INNER // SUBLANES),
        in_specs=[outer_spec, pl.BlockSpec((2 * n1, 2 * n1), lambda b, j: (0, 0))],
        out_specs=outer_spec,
        out_shape=jax.ShapeDtypeStruct((n_seq, 2, n1, FFT_INNER, width), F32),
        compiler_params=_params("parallel", "parallel"),
        name="fnet_outer_dft",
    )(v.reshape(n_seq, 2, n1, FFT_INNER, width), outer)
    group = min(SUBLANES, n1)
    f = pl.pallas_call(
        functools.partial(_fnet_inner_kernel, norm=1.0 / math.sqrt(seq * gw)),
        grid=(n_seq, n1 // group),
        in_specs=[pl.BlockSpec((1, 2, group, FFT_INNER, width), lambda b, k: (b, 0, k, 0, 0)),
                  pl.BlockSpec((group, FFT_INNER, 2 * FFT_INNER), lambda b, k: (k, 0, 0))],
        out_specs=pl.BlockSpec((1, FFT_INNER, group, width), lambda b, k: (b, 0, k, 0)),
        out_shape=jax.ShapeDtypeStruct((n_seq, FFT_INNER, n1, width), F32),
        compiler_params=_params("parallel", "parallel"),
        name="fnet_inner_dft",
    )(a, inner)
    return f.reshape(t, width)


def _rope_freqs(dim):
    return ROPE_THETA ** (-jnp.arange(0, dim, 2, dtype=F32) / dim)


def _pair_tables(ang_a, ang_b):
    ca, sa = jnp.cos(ang_a), jnp.sin(ang_a)
    zero = jnp.zeros_like(ca)
    if ang_b is None:
        cb, sb = zero, zero
    else:
        cb, sb = jnp.cos(ang_b), jnp.sin(ang_b)
    cos = jnp.concatenate([ca, cb, ca, cb], axis=1)
    sin = jnp.concatenate([-sa, -sb, sa, sb], axis=1)
    return cos, sin


def _pad_cols(w, width):
    return jnp.pad(w, ((0, 0), (0, width - w.shape[1])))


def _even_layer(x, h, mods, seq, w_in, pool_w, pool_scale, q_norm, wq_b, kv_norm, wkv_b, w_out, tabs):
    pool_width = pool_scale.shape[0]
    q_rank = q_norm.shape[0]
    kv_rank = kv_norm.shape[0]
    heads = wq_b.shape[1] // (MLA_NOPE + MLA_ROPE)
    half = MLA_ROPE // 2
    kr_lo = pool_width + q_rank + kv_rank
    w_in_p = jnp.concatenate([w_in[:, :kr_lo], _pad_cols(w_in[:, kr_lo:kr_lo + half], ROPE_SPLIT),
                              _pad_cols(w_in[:, kr_lo + half:], ROPE_SPLIT)], axis=1)
    w_in_p = _pad_cols(w_in_p, -(-w_in_p.shape[1] // MXU_WIDTH) * MXU_WIDTH).astype(BF16)
    wq = wq_b.reshape(q_rank, heads, MLA_NOPE + MLA_ROPE)
    spare = jnp.zeros((q_rank, heads, ROPE_SPLIT - half), wq.dtype)
    wq_pad = jnp.concatenate([wq[..., :MLA_NOPE], wq[..., MLA_NOPE:MLA_NOPE + half], spare,
                              wq[..., MLA_NOPE + half:], spare], axis=-1)
    wq_pad = wq_pad.reshape(q_rank, heads * MLA_QK_PAD).astype(BF16)

    z = _matmul(h, w_in_p, F32, tm_target=1024, tn_target=768)
    a = _pool(z, pool_w.astype(BF16), pool_scale, seq)
    scale = (MLA_NOPE + MLA_ROPE) ** -0.5 * LOG2E
    q = _mla_q(z, pool_width // q_rank, q_norm, wq_pad, tabs, seq, scale)
    k, v = _mla_kv(z, (pool_width + q_rank) // kv_rank, (pool_width + q_rank + kv_rank) // LANES,
                   kv_norm, wkv_b.astype(BF16), tabs, seq)
    attn = _flash(q, k, _transpose_values(v, heads), seq, n_groups=heads // 2, n_heads=2, dq=MLA_QK_PAD, shared_kv=False,
                  shift_col=MLA_NOPE + MLA_SHIFT_LANE)
    w_out = w_out.astype(BF16)
    return _matmul_resid([a, attn], [w_out[:pool_width], w_out[pool_width:]], x, mods, seq,
                         gate_row=2, tm_target=1024, tn_target=512)


def _odd_layer(x, h, mods, seq, w_in, q_norm, k_norm, fnet_w, w_out, tabs):
    fnet_width = fnet_w.shape[0]
    qkv_width = w_in.shape[1] - fnet_width
    kd = qkv_width // (GQA_GROUP + 2)
    qd = GQA_GROUP * kd
    w_in = w_in.astype(BF16)
    perm = _half_split_perm(GQA_HEAD_DIM)
    d_in = w_in.shape[0]
    w_q = w_in[:, :qd].reshape(d_in, qd // GQA_HEAD_DIM, GQA_HEAD_DIM)[:, :, perm].reshape(d_in, qd)
    w_k = w_in[:, qd:qd + kd].reshape(d_in, kd // GQA_HEAD_DIM, GQA_HEAD_DIM)[:, :, perm].reshape(d_in, kd)
    w_kv = jnp.concatenate([w_k, w_in[:, qd + kd:qd + 2 * kd]], axis=1)

    q = _gqa_qk(h, w_q, q_norm[perm], tabs, seq, GQA_HEAD_DIM ** -0.5 * LOG2E)
    k, v = _gqa_kv(h, w_kv, k_norm[perm], tabs, seq)
    u = _matmul(h, w_in[:, qd + 2 * kd:], F32)
    attn = _flash(q, k, _transpose_values(v, kd // GQA_HEAD_DIM), seq, n_groups=kd // GQA_HEAD_DIM, n_heads=GQA_GROUP,
                  dq=GQA_HEAD_DIM, shared_kv=True, shift_col=GQA_HEAD_DIM)
    f = _fourier(u, seq)
    ff = _matmul(f, fnet_w.astype(BF16), BF16)
    w_out = w_out.astype(BF16)
    return _matmul_resid([attn, ff], [w_out[:qd], w_out[qd:]], x, mods, seq,
                         gate_row=2, tm_target=1024, tn_target=512)


def _ffn(x, mods, seq, norm_w, w_gate, w_up, w_down):
    h = _normmod(x, norm_w, mods, seq, shift_row=3, scale_row=4)
    hid = _matmul_swiglu(h, w_gate.astype(BF16), w_up.astype(BF16))
    return _matmul_resid([hid], [w_down.astype(BF16)], x, mods, seq,
                         gate_row=5, tm_target=512, tn_target=512)


def kernel(x_prompt, x_sample, c_prompt, c_sample, mod_w, mod_b, norm_mix, norm_ffn, e_w_in, e_pool_w, e_pool_scale, e_q_norm, e_wq_b, e_kv_norm, e_wkv_b, e_w_out, o_w_in, o_q_norm, o_k_norm, o_fnet_w, o_w_out, ffn_gate, ffn_up, ffn_down, final_norm):
    n_prompt, seq, d = x_prompt.shape
    n_sample = x_sample.shape[0]
    assert x_sample.shape[1] == seq and seq % GRID_W == 0 and seq % FFT_INNER == 0
    n_seq = n_prompt + n_sample
    depth = mod_w.shape[0]

    c = jnp.concatenate([c_prompt, c_sample], axis=0)
    c_pad = jnp.pad(c, ((0, -n_seq % SUBLANES), (0, 0)))
    mods_all = _modulation(c_pad, mod_w, mod_b).reshape(depth, c_pad.shape[0], 6, d)

    pos = jnp.arange(seq)
    ang_1d = pos.astype(F32)[:, None] * _rope_freqs(MLA_ROPE)[None, :]
    ang_row = (pos // GRID_W).astype(F32)[:, None] * _rope_freqs(GQA_HEAD_DIM // 2)[None, :]
    ang_col = (pos % GRID_W).astype(F32)[:, None] * _rope_freqs(GQA_HEAD_DIM // 2)[None, :]
    mla_tabs = _pair_tables(ang_1d, None)
    gqa_tabs = _pair_tables(ang_row, ang_col)

    x, h = _stack_normmod(x_prompt.reshape(n_prompt * seq, d), x_sample.reshape(n_sample * seq, d),
                          norm_mix[0], mods_all[0], seq, shift_row=0, scale_row=1)
    for l in range(depth):
        mods = mods_all[l]
        i = l // 2
        if l > 0:
            h = _normmod(x, norm_mix[l], mods, seq, shift_row=0, scale_row=1)
        if l % 2 == 0:
            x = _even_layer(x, h, mods, seq, e_w_in[i], e_pool_w[i], e_pool_scale[i], e_q_norm[i],
                            e_wq_b[i], e_kv_norm[i], e_wkv_b[i], e_w_out[i], mla_tabs)
        else:
            x = _odd_layer(x, h, mods, seq, o_w_in[i], o_q_norm[i], o_k_norm[i], o_fnet_w[i],
                           o_w_out[i], gqa_tabs)
        x = _ffn(x, mods, seq, norm_ffn[l], ffn_gate[l], ffn_up[l], ffn_down[l])

    y_prompt = _final_norm(x, final_norm, seq, 0, n_prompt).reshape(n_prompt, seq, d)
    y_sample = _final_norm(x, final_norm, seq, n_prompt, n_sample).reshape(n_sample, seq, d)
    return (y_prompt, y_sample)
```

```python
import functools
import math

import jax
import jax.numpy as jnp
from jax import lax
from jax.experimental import pallas as pl
from jax.experimental.pallas import tpu as pltpu

F32 = jnp.float32
BF16 = jnp.bfloat16

EPS = 1e-6
ROPE_THETA = 10000.0
GRID_W = 64
POOL_WINDOWS = (2, 4, 8, 16)
MLA_NOPE = 128
MLA_ROPE = 64
MLA_V = 128
MLA_QK_PAD = 256
MLA_SHIFT_LANE = 32
GQA_HEAD_DIM = 128
GQA_GROUP = 4
FNET_GROUPS = 4
LOG2E = 1.4426950408889634

LANES = 128
SUBLANES = 8
VMEM_LIMIT_BYTES = 52 * 1024 * 1024

HIGHEST = lax.Precision.HIGHEST


def _pick(n, target, quantum=LANES):
    best = None
    t = quantum
    while t <= min(n, target):
        if n % t == 0:
            best = t
        t += quantum
    if best is None:
        return n
    return best


def _params(*sem, vmem_limit=VMEM_LIMIT_BYTES):
    return pltpu.CompilerParams(dimension_semantics=sem, vmem_limit_bytes=vmem_limit)


ROPE_SPLIT = LANES // 2


def _rot_pair(x, c, s):
    return x * c + pltpu.roll(x, ROPE_SPLIT, 1) * s


def _half_split_perm(width):
    q = width // 4
    idx = jnp.arange(width).reshape(2, 2, q)
    return idx.transpose(1, 0, 2).reshape(width)


def _mod_kernel(c_ref, w_ref, b_ref, o_ref):
    c = c_ref[...]
    act = (c / (1.0 + jnp.exp(-c))).astype(BF16)
    w = w_ref[0].astype(BF16)
    o_ref[0] = jnp.dot(act, w, preferred_element_type=F32) + b_ref[0]


def _modulation(c_pad, mod_w, mod_b):
    depth, d, n = mod_w.shape
    rows = c_pad.shape[0]
    tn = _pick(n, 512)
    return pl.pallas_call(
        _mod_kernel,
        grid=(depth, n // tn),
        in_specs=[
            pl.BlockSpec((rows, d), lambda l, j: (0, 0)),
            pl.BlockSpec((1, d, tn), lambda l, j: (l, 0, j)),
            pl.BlockSpec((1, 1, tn), lambda l, j: (l, 0, j)),
        ],
        out_specs=pl.BlockSpec((1, rows, tn), lambda l, j: (l, 0, j)),
        out_shape=jax.ShapeDtypeStruct((depth, rows, n), F32),
        compiler_params=_params("parallel", "parallel"),
        name="modulation",
    )(c_pad, mod_w, mod_b.reshape(depth, 1, n))


def _normmod_kernel(x_ref, w_ref, m_ref, o_ref, *, shift_row, scale_row):
    x = x_ref[...]
    y = x * lax.rsqrt(jnp.mean(x * x, axis=-1, keepdims=True) + EPS) * w_ref[...]
    sc = m_ref[0, scale_row:scale_row + 1, :]
    sh = m_ref[0, shift_row:shift_row + 1, :]
    o_ref[...] = (y * (1.0 + sc) + sh).astype(o_ref.dtype)


def _norm_kernel(x_ref, w_ref, o_ref):
    x = x_ref[...]
    y = x * lax.rsqrt(jnp.mean(x * x, axis=-1, keepdims=True) + EPS) * w_ref[...]
    o_ref[...] = y.astype(o_ref.dtype)


def _stack_normmod_kernel(xa_ref, xb_ref, w_ref, m_ref, x_ref, h_ref, *, n_first, shift_row, scale_row):
    def emit(src_ref):
        x = src_ref[...]
        x_ref[...] = x
        y = x * lax.rsqrt(jnp.mean(x * x, axis=-1, keepdims=True) + EPS) * w_ref[...]
        sc = m_ref[0, scale_row:scale_row + 1, :]
        sh = m_ref[0, shift_row:shift_row + 1, :]
        h_ref[...] = (y * (1.0 + sc) + sh).astype(h_ref.dtype)

    first = pl.program_id(0) < n_first
    pl.when(first)(lambda: emit(xa_ref))
    pl.when(jnp.logical_not(first))(lambda: emit(xb_ref))


def _stack_normmod(xa, xb, w, mods, seq, shift_row, scale_row):
    d = xa.shape[1]
    t = xa.shape[0] + xb.shape[0]
    tm = _pick(seq, 256, SUBLANES)
    per_seq = seq // tm
    n_first = xa.shape[0] // tm
    return pl.pallas_call(
        functools.partial(_stack_normmod_kernel, n_first=n_first, shift_row=shift_row, scale_row=scale_row),
        grid=(t // tm,),
        in_specs=[
            pl.BlockSpec((tm, d), lambda i: (jnp.minimum(i, n_first - 1), 0)),
            pl.BlockSpec((tm, d), lambda i: (jnp.maximum(i - n_first, 0), 0)),
            pl.BlockSpec((1, d), lambda i: (0, 0)),
            pl.BlockSpec((1, 6, d), lambda i: (i // per_seq, 0, 0)),
        ],
        out_specs=[pl.BlockSpec((tm, d), lambda i: (i, 0)), pl.BlockSpec((tm, d), lambda i: (i, 0))],
        out_shape=[jax.ShapeDtypeStruct((t, d), F32), jax.ShapeDtypeStruct((t, d), BF16)],
        compiler_params=_params("arbitrary"),
        name="stack_normmod",
    )(xa, xb, w.reshape(1, d), mods)


def _normmod(x, w, mods, seq, shift_row, scale_row):
    t, d = x.shape
    tm = _pick(seq, 256, SUBLANES)
    per_seq = seq // tm
    return pl.pallas_call(
        functools.partial(_normmod_kernel, shift_row=shift_row, scale_row=scale_row),
        grid=(t // tm,),
        in_specs=[
            pl.BlockSpec((tm, d), lambda i: (i, 0)),
            pl.BlockSpec((1, d), lambda i: (0, 0)),
            pl.BlockSpec((1, 6, d), lambda i: (i // per_seq, 0, 0)),
        ],
        out_specs=pl.BlockSpec((tm, d), lambda i: (i, 0)),
        out_shape=jax.ShapeDtypeStruct((t, d), BF16),
        compiler_params=_params("parallel"),
        name="normmod",
    )(x, w.reshape(1, d), mods)


def _final_norm(x, w, seq, first_seq, n_seq):
    d = x.shape[1]
    tm = _pick(seq, 256, SUBLANES)
    first_block = first_seq * (seq // tm)
    return pl.pallas_call(
        _norm_kernel,
        grid=(n_seq * seq // tm,),
        in_specs=[pl.BlockSpec((tm, d), lambda i: (first_block + i, 0)),
                  pl.BlockSpec((1, d), lambda i: (0, 0))],
        out_specs=pl.BlockSpec((tm, d), lambda i: (i, 0)),
        out_shape=jax.ShapeDtypeStruct((n_seq * seq, d), F32),
        compiler_params=_params("parallel"),
        name="final_norm",
    )(x, w.reshape(1, d))


def _mm_plain_kernel(a_ref, w_ref, o_ref):
    a = a_ref[...].astype(w_ref.dtype)
    o_ref[...] = jnp.dot(a, w_ref[...], preferred_element_type=F32).astype(o_ref.dtype)


def _matmul(a, w, out_dtype, tm_target=1024, tn_target=512):
    m, k = a.shape
    n = w.shape[1]
    tm = _pick(m, tm_target, SUBLANES)
    tn = _pick(n, tn_target)
    return pl.pallas_call(
        _mm_plain_kernel,
        grid=(m // tm, n // tn),
        in_specs=[pl.BlockSpec((tm, k), lambda i, j: (i, 0)),
                  pl.BlockSpec((k, tn), lambda i, j: (0, j))],
        out_specs=pl.BlockSpec((tm, tn), lambda i, j: (i, j)),
        out_shape=jax.ShapeDtypeStruct((m, n), out_dtype),
        compiler_params=_params("parallel", "parallel"),
        name="matmul",
    )(a, w)


def _mm_resid_kernel(*refs, n_a, gate_row):
    a_refs = refs[:n_a]
    w_refs = refs[n_a:2 * n_a]
    x_ref, m_ref, o_ref = refs[2 * n_a:]
    acc = jnp.dot(a_refs[0][...], w_refs[0][...], preferred_element_type=F32)
    for a_ref, w_ref in zip(a_refs[1:], w_refs[1:]):
        acc = acc + jnp.dot(a_ref[...], w_ref[...], preferred_element_type=F32)
    o_ref[...] = x_ref[...] + m_ref[0, gate_row:gate_row + 1, :] * acc


def _weight_spec(w, tn, layer):
    if w.ndim == 3:
        return pl.BlockSpec((None, w.shape[1], tn), lambda i, j: (layer, 0, j))
    return pl.BlockSpec((w.shape[0], tn), lambda i, j: (0, j))


def _matmul_resid(a_list, w_list, x, mods, seq, gate_row, tm_target, tn_target, layer=None):
    t, n = x.shape
    tm = _pick(seq, tm_target, SUBLANES)
    tn = _pick(n, tn_target)
    per_seq = seq // tm
    n_a = len(a_list)
    in_specs = [pl.BlockSpec((tm, a.shape[1]), lambda i, j: (i, 0)) for a in a_list]
    in_specs += [_weight_spec(w, tn, layer) for w in w_list]
    in_specs += [pl.BlockSpec((tm, tn), lambda i, j: (i, j)),
                 pl.BlockSpec((1, 6, tn), lambda i, j: (i // per_seq, 0, j))]
    return pl.pallas_call(
        functools.partial(_mm_resid_kernel, n_a=n_a, gate_row=gate_row),
        grid=(t // tm, n // tn),
        in_specs=in_specs,
        out_specs=pl.BlockSpec((tm, tn), lambda i, j: (i, j)),
        out_shape=jax.ShapeDtypeStruct((t, n), F32),
        compiler_params=_params("parallel", "parallel"),
        name="matmul_resid",
    )(*a_list, *w_list, x, mods)


def _mm_swiglu_kernel(a_ref, wg_ref, wu_ref, o_ref):
    a = a_ref[...]
    g = jnp.dot(a, wg_ref[...], preferred_element_type=F32)
    u = jnp.dot(a, wu_ref[...], preferred_element_type=F32)
    o_ref[...] = (g / (1.0 + jnp.exp(-g)) * u).astype(o_ref.dtype)


def _matmul_swiglu(a, wg, wu, layer=None, tm_target=2048, tn_target=256):
    m, k = a.shape
    n = wg.shape[-1]
    tm = _pick(m, tm_target, SUBLANES)
    tn = _pick(n, tn_target)
    return pl.pallas_call(
        _mm_swiglu_kernel,
        grid=(m // tm, n // tn),
        in_specs=[pl.BlockSpec((tm, k), lambda i, j: (i, 0)),
                  _weight_spec(wg, tn, layer),
                  _weight_spec(wu, tn, layer)],
        out_specs=pl.BlockSpec((tm, tn), lambda i, j: (i, j)),
        out_shape=jax.ShapeDtypeStruct((m, n), BF16),
        compiler_params=_params("parallel", "parallel"),
        name="matmul_swiglu",
    )(a, wg, wu)


def _mla_q_kernel(a_ref, nw_ref, w_ref, c_ref, s_ref, o_ref, *, scale):
    a = a_ref[...]
    nrm = (a * lax.rsqrt(jnp.mean(a * a, axis=-1, keepdims=True) + EPS) * nw_ref[...]).astype(BF16)
    c, s = c_ref[...], s_ref[...]
    for h in range(o_ref.shape[1] // MLA_QK_PAD):
        lo = h * MLA_QK_PAD
        acc = jnp.dot(nrm, w_ref[:, lo:lo + MLA_QK_PAD], preferred_element_type=F32)
        o_ref[:, lo:lo + MLA_NOPE] = (acc[:, :MLA_NOPE] * scale).astype(o_ref.dtype)
        o_ref[:, lo + MLA_NOPE:lo + MLA_QK_PAD] = (_rot_pair(acc[:, MLA_NOPE:], c, s) * scale).astype(o_ref.dtype)


def _mla_q(z, col_block, q_norm, wq_pad, tabs, seq, scale):
    t = z.shape[0]
    rank, n = wq_pad.shape
    tm = _pick(seq, 512, SUBLANES)
    per_seq = seq // tm
    tab_spec = pl.BlockSpec((tm, LANES), lambda i: (i % per_seq, 0))
    return pl.pallas_call(
        functools.partial(_mla_q_kernel, scale=scale),
        grid=(t // tm,),
        in_specs=[pl.BlockSpec((tm, rank), lambda i: (i, col_block)),
                  pl.BlockSpec((1, rank), lambda i: (0, 0)),
                  pl.BlockSpec((rank, n), lambda i: (0, 0)),
                  tab_spec, tab_spec],
        out_specs=pl.BlockSpec((tm, n), lambda i: (i, 0)),
        out_shape=jax.ShapeDtypeStruct((t, n), BF16),
        compiler_params=_params("parallel"),
        name="mla_q_proj",
    )(z, q_norm.reshape(1, rank), wq_pad, *tabs)


def _mla_kv_kernel(a_ref, kr_ref, nw_ref, w_ref, c_ref, s_ref, k_ref, v_ref):
    a = a_ref[...]
    nrm = (a * lax.rsqrt(jnp.mean(a * a, axis=-1, keepdims=True) + EPS) * nw_ref[...]).astype(BF16)
    kr = _rot_pair(kr_ref[...], c_ref[...], s_ref[...])
    lane = lax.broadcasted_iota(jnp.int32, kr.shape, 1)
    kr = jnp.where(lane == MLA_SHIFT_LANE, 1.0, kr).astype(k_ref.dtype)
    ones = _ones_col(kr.shape[0], v_ref.dtype)
    width = MLA_NOPE + MLA_V
    for h in range(w_ref.shape[1] // width):
        acc = jnp.dot(nrm, w_ref[:, h * width:(h + 1) * width], preferred_element_type=F32)
        k_ref[:, h * MLA_QK_PAD:h * MLA_QK_PAD + MLA_NOPE] = acc[:, :MLA_NOPE].astype(k_ref.dtype)
        k_ref[:, h * MLA_QK_PAD + MLA_NOPE:(h + 1) * MLA_QK_PAD] = kr
        v_ref[:, h * ATTN_PAD:h * ATTN_PAD + MLA_V] = acc[:, MLA_NOPE:].astype(v_ref.dtype)
        v_ref[:, h * ATTN_PAD + MLA_V:(h + 1) * ATTN_PAD] = ones


def _mla_kv(z, kv_col_block, kr_col_block, kv_norm, wkv, tabs, seq):
    t = z.shape[0]
    rank, n = wkv.shape
    heads = n // (MLA_NOPE + MLA_V)
    tm = _pick(seq, 512, SUBLANES)
    per_seq = seq // tm
    tab_spec = pl.BlockSpec((tm, LANES), lambda i: (i % per_seq, 0))
    return pl.pallas_call(
        _mla_kv_kernel,
        grid=(t // tm,),
        in_specs=[pl.BlockSpec((tm, rank), lambda i: (i, kv_col_block)),
                  pl.BlockSpec((tm, LANES), lambda i: (i, kr_col_block)),
                  pl.BlockSpec((1, rank), lambda i: (0, 0)),
                  pl.BlockSpec((rank, n), lambda i: (0, 0)),
                  tab_spec, tab_spec],
        out_specs=[pl.BlockSpec((tm, heads * MLA_QK_PAD), lambda i: (i, 0)),
                   pl.BlockSpec((tm, heads * ATTN_PAD), lambda i: (i, 0))],
        out_shape=[jax.ShapeDtypeStruct((t, heads * MLA_QK_PAD), BF16),
                   jax.ShapeDtypeStruct((t, heads * ATTN_PAD), BF16)],
        compiler_params=_params("parallel"),
        name="mla_kv_proj",
    )(z, z, kv_norm.reshape(1, rank), wkv, *tabs)


MXU_WIDTH = 256


def _head_norm_rope(x, nw, c, s):
    y = x * lax.rsqrt(jnp.mean(x * x, axis=-1, keepdims=True) + EPS) * nw
    return _rot_pair(y, c, s)


def _gqa_qk_kernel(a_ref, w_ref, nw_ref, c_ref, s_ref, o_ref, *, scale):
    a = a_ref[...]
    nw, c, s = nw_ref[...], c_ref[...], s_ref[...]
    for p in range(o_ref.shape[1] // MXU_WIDTH):
        acc = jnp.dot(a, w_ref[:, p * MXU_WIDTH:(p + 1) * MXU_WIDTH], preferred_element_type=F32)
        for h in range(MXU_WIDTH // GQA_HEAD_DIM):
            lo = h * GQA_HEAD_DIM
            y = _head_norm_rope(acc[:, lo:lo + GQA_HEAD_DIM], nw, c, s) * scale
            o_ref[:, p * MXU_WIDTH + lo:p * MXU_WIDTH + lo + GQA_HEAD_DIM] = y.astype(o_ref.dtype)


def _gqa_qk(a, w, head_norm, tabs, seq, scale):
    t, k = a.shape
    n = w.shape[1]
    tm = _pick(seq, 1024, SUBLANES)
    tn = _pick(n, 512)
    per_seq = seq // tm
    tab_spec = pl.BlockSpec((tm, LANES), lambda i, j: (i % per_seq, 0))
    return pl.pallas_call(
        functools.partial(_gqa_qk_kernel, scale=scale),
        grid=(t // tm, n // tn),
        in_specs=[pl.BlockSpec((tm, k), lambda i, j: (i, 0)),
                  pl.BlockSpec((k, tn), lambda i, j: (0, j)),
                  pl.BlockSpec((1, GQA_HEAD_DIM), lambda i, j: (0, 0)),
                  tab_spec, tab_spec],
        out_specs=pl.BlockSpec((tm, tn), lambda i, j: (i, j)),
        out_shape=jax.ShapeDtypeStruct((t, n), BF16),
        compiler_params=_params("parallel", "parallel"),
        name="gqa_qk_proj",
    )(a, w, head_norm.reshape(1, GQA_HEAD_DIM), *tabs)


def _gqa_kv_kernel(a_ref, w_ref, nw_ref, c_ref, s_ref, k_ref, v_ref):
    a = a_ref[...]
    nw, c, s = nw_ref[...], c_ref[...], s_ref[...]
    kd = w_ref.shape[1] // 2
    ones = _ones_col(a.shape[0], k_ref.dtype)
    per_dot = MXU_WIDTH // GQA_HEAD_DIM
    for p in range(w_ref.shape[1] // MXU_WIDTH):
        acc = jnp.dot(a, w_ref[:, p * MXU_WIDTH:(p + 1) * MXU_WIDTH], preferred_element_type=F32)
        for h in range(per_dot):
            x = acc[:, h * GQA_HEAD_DIM:(h + 1) * GQA_HEAD_DIM]
            col = p * MXU_WIDTH + h * GQA_HEAD_DIM
            if col < kd:
                head = col // GQA_HEAD_DIM
                k_ref[:, head * ATTN_PAD:head * ATTN_PAD + GQA_HEAD_DIM] = (
                    _head_norm_rope(x, nw, c, s).astype(k_ref.dtype))
                k_ref[:, head * ATTN_PAD + GQA_HEAD_DIM:(head + 1) * ATTN_PAD] = ones
            else:
                head = (col - kd) // GQA_HEAD_DIM
                v_ref[:, head * ATTN_PAD:head * ATTN_PAD + GQA_HEAD_DIM] = x.astype(v_ref.dtype)
                v_ref[:, head * ATTN_PAD + GQA_HEAD_DIM:(head + 1) * ATTN_PAD] = ones


def _gqa_kv(a, w_kv, k_norm, tabs, seq):
    t, k = a.shape
    n = w_kv.shape[1]
    heads = n // (2 * GQA_HEAD_DIM)
    tm = _pick(seq, 512, SUBLANES)
    per_seq = seq // tm
    tab_spec = pl.BlockSpec((tm, LANES), lambda i: (i % per_seq, 0))
    out_spec = pl.BlockSpec((tm, heads * ATTN_PAD), lambda i: (i, 0))
    out_shape = jax.ShapeDtypeStruct((t, heads * ATTN_PAD), BF16)
    return pl.pallas_call(
        _gqa_kv_kernel,
        grid=(t // tm,),
        in_specs=[pl.BlockSpec((tm, k), lambda i: (i, 0)),
                  pl.BlockSpec((k, n), lambda i: (0, 0)),
                  pl.BlockSpec((1, GQA_HEAD_DIM), lambda i: (0, 0)),
                  tab_spec, tab_spec],
        out_specs=[out_spec, out_spec],
        out_shape=[out_shape, out_shape],
        compiler_params=_params("parallel"),
        name="gqa_kv_proj",
    )(a, w_kv, k_norm.reshape(1, GQA_HEAD_DIM), *tabs)


ATTN_PAD = 256
FAST_PATH_MIN_SUM = 2.0 ** -60
BOUND_MARGIN = 1.0 + 2.0 ** -6
NT_DIMS = (((1,), (1,)), ((), ()))


def _ones_col(rows, dtype):
    lane = lax.broadcasted_iota(jnp.int32, (rows, LANES), 1)
    return jnp.where(lane == 0, 1.0, 0.0).astype(dtype)


VT_ROWS = 144


def _flash_kernel(q_ref, k_ref, vt_ref, o_ref, kmax_scr, acct_scr, acc_scr, m_scr, l_scr, *,
                  n_heads, dq, dv, shared_kv, shift_col, chunk):
    seq = k_ref.shape[0]
    tq = q_ref.shape[0]
    n_chunks = seq // chunk
    n_kv = 1 if shared_kv else n_heads

    def kv_chunk(c, hk):
        r0 = c * chunk if isinstance(c, int) else pl.multiple_of(c * chunk, chunk)
        kc = k_ref[pl.ds(r0, chunk), hk * ATTN_PAD:(hk + 1) * ATTN_PAD]
        vtc = vt_ref[hk * VT_ROWS:(hk + 1) * VT_ROWS, pl.ds(r0, chunk)]
        return kc, vtc

    @pl.when(pl.program_id(2) == 0)
    def _():
        for hk in range(n_kv):
            def norm_body(c, best):
                kf = kv_chunk(c, hk)[0].astype(F32)
                row = jnp.sum(kf * kf, axis=-1, keepdims=True)
                return jnp.maximum(best, jnp.max(row, axis=0, keepdims=True))
            best = lax.fori_loop(0, n_chunks, norm_body, jnp.zeros((1, 1), F32))
            kmax_scr[hk] = jnp.broadcast_to(jnp.sqrt(best), kmax_scr.shape[1:])

    lane = lax.broadcasted_iota(jnp.int32, (tq, LANES), 1)
    ok = None
    for h in range(n_heads):
        hk = 0 if shared_kv else h
        q = q_ref[:, h * dq:(h + 1) * dq]
        qf = q.astype(F32)
        shift = jnp.sqrt(jnp.sum(qf * qf, axis=-1, keepdims=True)) * kmax_scr[hk][0:1, 0:1] * BOUND_MARGIN
        hi = qf[:, LANES:] if dq > LANES else jnp.zeros((tq, LANES), F32)
        hi = jnp.where(lane == shift_col - LANES, -shift, hi).astype(q.dtype)
        qx = jnp.concatenate([q[:, :LANES], hi], axis=1)
        for c in range(n_chunks):
            kc, vtc = kv_chunk(c, hk)
            st = lax.dot_general(kc, qx, NT_DIMS, preferred_element_type=F32)
            part = jnp.dot(vtc, jnp.exp2(st).astype(vtc.dtype), preferred_element_type=F32)
            if c == 0:
                acct_scr[...] = part
            else:
                acct_scr[...] += part
        acct = acct_scr[...]
        l = acct[dv:dv + 1, :]
        o_ref[:, h * dv:(h + 1) * dv] = (acct[:dv, :] / l).T.astype(o_ref.dtype)
        head_ok = jnp.min(l) >= FAST_PATH_MIN_SUM
        ok = head_ok if ok is None else jnp.logical_and(ok, head_ok)

    @pl.when(jnp.logical_not(ok))
    def _():
        for h in range(n_heads):
            hk = 0 if shared_kv else h
            q = q_ref[:, h * dq:(h + 1) * dq]
            m_scr[...] = jnp.full(m_scr.shape, -jnp.inf, F32)
            l_scr[...] = jnp.zeros(l_scr.shape, F32)
            acc_scr[...] = jnp.zeros(acc_scr.shape, F32)

            def exact_body(c, carry):
                kc, vtc = kv_chunk(c, hk)
                s = lax.dot_general(q, kc[:, :dq], NT_DIMS, preferred_element_type=F32)
                m_prev = m_scr[:, 0:1]
                m_new = jnp.maximum(m_prev, jnp.max(s, axis=-1, keepdims=True))
                alpha = jnp.exp2(m_prev - m_new)
                p = jnp.exp2(s - m_new)
                l_new = alpha * l_scr[:, 0:1] + jnp.sum(p, axis=-1, keepdims=True)
                pv = lax.dot_general(p.astype(vtc.dtype), vtc[:dv, :], NT_DIMS, preferred_element_type=F32)
                acc_scr[...] = alpha * acc_scr[...] + pv
                m_scr[...] = jnp.broadcast_to(m_new, m_scr.shape)
                l_scr[...] = jnp.broadcast_to(l_new, l_scr.shape)
                return carry

            lax.fori_loop(0, n_chunks, exact_body, 0)
            o_ref[:, h * dv:(h + 1) * dv] = (acc_scr[...] / l_scr[:, 0:1]).astype(o_ref.dtype)


def _flash(q, k, vt, seq, n_groups, n_heads, dq, shared_kv, shift_col, tq_target=512, chunk_target=1024):
    t = q.shape[0]
    n_seq = t // seq
    dv = ATTN_PAD // 2
    tq = _pick(seq, tq_target, LANES)
    chunk = _pick(seq, chunk_target, LANES)
    qb = seq // tq
    n_kv = 1 if shared_kv else n_heads
    return pl.pallas_call(
        functools.partial(_flash_kernel, n_heads=n_heads, dq=dq, dv=dv, shared_kv=shared_kv,
                          shift_col=shift_col, chunk=chunk),
        grid=(n_seq, n_groups, qb),
        in_specs=[pl.BlockSpec((tq, n_heads * dq), lambda b, g, i: (b * qb + i, g)),
                  pl.BlockSpec((seq, n_kv * ATTN_PAD), lambda b, g, i: (b, g)),
                  pl.BlockSpec((n_kv * VT_ROWS, seq), lambda b, g, i: (g, b))],
        out_specs=pl.BlockSpec((tq, n_heads * dv), lambda b, g, i: (b * qb + i, g)),
        out_shape=jax.ShapeDtypeStruct((t, n_groups * n_heads * dv), BF16),
        scratch_shapes=[pltpu.VMEM((n_kv, SUBLANES, LANES), F32),
                        pltpu.VMEM((VT_ROWS, tq), F32),
                        pltpu.VMEM((tq, dv), F32),
                        pltpu.VMEM((tq, LANES), F32),
                        pltpu.VMEM((tq, LANES), F32)],
        compiler_params=_params("parallel", "parallel", "arbitrary"),
        name="flash_attention",
    )(q, k, vt)


def _transpose_values(v, n_heads):
    t = v.shape[0]
    vt = v.reshape(t, n_heads, ATTN_PAD)[:, :, :VT_ROWS]
    return vt.transpose(1, 2, 0).reshape(n_heads * VT_ROWS, t)


POOL_HALO = 16


def _pool_kernel(prev_ref, cur_ref, next_ref, w_ref, s_ref, o_ref, ext_ref, *, seq, group):
    tm = cur_ref.shape[0]
    ext_ref[0:POOL_HALO, :] = prev_ref[...]
    ext_ref[POOL_HALO:POOL_HALO + tm, :] = cur_ref[...]
    ext_ref[POOL_HALO + tm:, :] = next_ref[...]
    pos = (pl.program_id(0) * tm) % seq + lax.broadcasted_iota(jnp.int32, (tm, 1), 0)
    for gi, win in enumerate(POOL_WINDOWS):
        cols = slice(gi * group, (gi + 1) * group)
        total = jnp.zeros((tm, group), F32)
        count = jnp.zeros((tm, 1), F32)
        for off in range(-(win // 2), win - win // 2):
            valid = jnp.logical_and(pos + off >= 0, pos + off < seq)
            term = ext_ref[POOL_HALO + off:POOL_HALO + off + tm, cols]
            total = total + jnp.where(valid, term, 0.0)
            count = count + valid.astype(F32)
        diff = (total / count - cur_ref[:, cols]).astype(BF16)
        y = jnp.dot(diff, w_ref[gi], preferred_element_type=F32) * s_ref[:, cols]
        o_ref[:, cols] = y.astype(o_ref.dtype)


def _pool(z, pool_w, pool_scale, seq):
    t = z.shape[0]
    n_groups, group, _ = pool_w.shape
    width = n_groups * group
    tm = _pick(seq, 256, POOL_HALO)
    halo_blocks = tm // POOL_HALO
    last_halo = t // POOL_HALO - 1
    return pl.pallas_call(
        functools.partial(_pool_kernel, seq=seq, group=group),
        grid=(t // tm,),
        in_specs=[
            pl.BlockSpec((POOL_HALO, width), lambda i: (jnp.maximum(i * halo_blocks - 1, 0), 0)),
            pl.BlockSpec((tm, width), lambda i: (i, 0)),
            pl.BlockSpec((POOL_HALO, width), lambda i: (jnp.minimum((i + 1) * halo_blocks, last_halo), 0)),
            pl.BlockSpec((n_groups, group, group), lambda i: (0, 0, 0)),
            pl.BlockSpec((1, width), lambda i: (0, 0)),
        ],
        out_specs=pl.BlockSpec((tm, width), lambda i: (i, 0)),
        out_shape=jax.ShapeDtypeStruct((t, width), BF16),
        scratch_shapes=[pltpu.VMEM((tm + 2 * POOL_HALO, width), F32)],
        compiler_params=_params("parallel"),
        name="pool_mixer",
    )(z, z, z, pool_w, pool_scale.reshape(1, width))


FFT_INNER = 128


def _fnet_chan_kernel(u_ref, m_ref, o_ref):
    gw = u_ref.shape[1]
    res = jnp.dot(u_ref[...], m_ref[...], precision=HIGHEST, preferred_element_type=F32)
    o_ref[0, 0] = res[:, :gw]
    o_ref[0, 1] = res[:, gw:]


def _fnet_outer_kernel(v_ref, m_ref, o_ref):
    two, n1, rows, width = v_ref.shape[1:]
    m = m_ref[...]
    for r in range(rows):
        x = v_ref[0, :, :, r, :].reshape(two * n1, width)
        y = jnp.dot(m, x, precision=HIGHEST, preferred_element_type=F32)
        o_ref[0, :, :, r, :] = y.reshape(two, n1, width)


def _fnet_inner_kernel(a_ref, g_ref, o_ref, *, norm):
    two, group, inner, width = a_ref.shape[1:]
    for r in range(group):
        x = a_ref[0, :, r].reshape(two * inner, width)
        y = jnp.dot(g_ref[r], x, precision=HIGHEST, preferred_element_type=F32)
        o_ref[0, :, r, :] = (y * norm).astype(o_ref.dtype)


def _fnet_tables(seq, gw):
    n1 = seq // FFT_INNER
    ch = jnp.arange(gw, dtype=jnp.int32)
    ang = (2.0 * math.pi / gw) * ((ch[:, None] * ch[None, :]) % gw).astype(F32)
    chan = jnp.concatenate([jnp.cos(ang), -jnp.sin(ang)], axis=1)
    i1 = jnp.arange(n1, dtype=jnp.int32)
    ang = (2.0 * math.pi / n1) * ((i1[:, None] * i1[None, :]) % n1).astype(F32)
    c, s = jnp.cos(ang), jnp.sin(ang)
    outer = jnp.concatenate([jnp.concatenate([c, s], axis=1),
                             jnp.concatenate([-s, c], axis=1)], axis=0)
    k2 = jnp.arange(FFT_INNER, dtype=jnp.int32)
    kk = i1[:, None, None] + n1 * k2[None, :, None]
    ang = (2.0 * math.pi / seq) * ((kk * k2[None, None, :]) % seq).astype(F32)
    inner = jnp.concatenate([jnp.cos(ang), jnp.sin(ang)], axis=2)
    return chan, outer, inner


def _fourier(u, seq):
    t, width = u.shape
    n_seq = t // seq
    gw = width // FNET_GROUPS
    n1 = seq // FFT_INNER
    chan, outer, inner = _fnet_tables(seq, gw)
    tm = _pick(seq, 512, SUBLANES)
    per_seq = seq // tm
    v = pl.pallas_call(
        _fnet_chan_kernel,
        grid=(t // tm, FNET_GROUPS),
        in_specs=[pl.BlockSpec((tm, gw), lambda i, g: (i, g)),
                  pl.BlockSpec((gw, 2 * gw), lambda i, g: (0, 0))],
        out_specs=pl.BlockSpec((1, 2, tm, gw), lambda i, g: (i // per_seq, 0, i % per_seq, g)),
        out_shape=jax.ShapeDtypeStruct((n_seq, 2, seq, width), F32),
        compiler_params=_params("parallel", "parallel"),
        name="fnet_channel_dft",
    )(u, chan)
    outer_spec = pl.BlockSpec((1, 2, n1, SUBLANES, width), lambda b, j: (b, 0, 0, j, 0))
    a = pl.pallas_call(
        _fnet_outer_kernel,
        grid=(n_seq, FFT_INNER // SUBLANES),
        in_specs=[outer_spec, pl.BlockSpec((2 * n1, 2 * n1), lambda b, j: (0, 0))],
        out_specs=outer_spec,
        out_shape=jax.ShapeDtypeStruct((n_seq, 2, n1, FFT_INNER, width), F32),
        compiler_params=_params("parallel", "parallel"),
        name="fnet_outer_dft",
    )(v.reshape(n_seq, 2, n1, FFT_INNER, width), outer)
    group = min(SUBLANES, n1)
    f = pl.pallas_call(
        functools.partial(_fnet_inner_kernel, norm=1.0 / math.sqrt(seq * gw)),
        grid=(n_seq, n1 // group),
        in_specs=[pl.BlockSpec((1, 2, group, FFT_INNER, width), lambda b, k: (b, 0, k, 0, 0)),
                  pl.BlockSpec((group, FFT_INNER, 2 * FFT_INNER), lambda b, k: (k, 0, 0))],
        out_specs=pl.BlockSpec((1, FFT_INNER, group, width), lambda b, k: (b, 0, k, 0)),
        out_shape=jax.ShapeDtypeStruct((n_seq, FFT_INNER, n1, width), F32),
        compiler_params=_params("parallel", "parallel"),
        name="fnet_inner_dft",
    )(a, inner)
    return f.reshape(t, width)


def _rope_freqs(dim):
    return ROPE_THETA ** (-jnp.arange(0, dim, 2, dtype=F32) / dim)


def _pair_tables(ang_a, ang_b):
    ca, sa = jnp.cos(ang_a), jnp.sin(ang_a)
    zero = jnp.zeros_like(ca)
    if ang_b is None:
        cb, sb = zero, zero
    else:
        cb, sb = jnp.cos(ang_b), jnp.sin(ang_b)
    cos = jnp.concatenate([ca, cb, ca, cb], axis=1)
    sin = jnp.concatenate([-sa, -sb, sa, sb], axis=1)
    return cos, sin


def _pad_cols(w, width):
    return jnp.pad(w, ((0, 0), (0, width - w.shape[1])))


def _even_layer(x, h, mods, seq, w_in, pool_w, pool_scale, q_norm, wq_b, kv_norm, wkv_b, w_out, tabs):
    pool_width = pool_scale.shape[0]
    q_rank = q_norm.shape[0]
    kv_rank = kv_norm.shape[0]
    heads = wq_b.shape[1] // (MLA_NOPE + MLA_ROPE)
    half = MLA_ROPE // 2
    kr_lo = pool_width + q_rank + kv_rank
    w_in_p = jnp.concatenate([w_in[:, :kr_lo], _pad_cols(w_in[:, kr_lo:kr_lo + half], ROPE_SPLIT),
                              _pad_cols(w_in[:, kr_lo + half:], ROPE_SPLIT)], axis=1)
    w_in_p = _pad_cols(w_in_p, -(-w_in_p.shape[1] // MXU_WIDTH) * MXU_WIDTH).astype(BF16)
    wq = wq_b.reshape(q_rank, heads, MLA_NOPE + MLA_ROPE)
    spare = jnp.zeros((q_rank, heads, ROPE_SPLIT - half), wq.dtype)
    wq_pad = jnp.concatenate([wq[..., :MLA_NOPE], wq[..., MLA_NOPE:MLA_NOPE + half], spare,
                              wq[..., MLA_NOPE + half:], spare], axis=-1)
    wq_pad = wq_pad.reshape(q_rank, heads * MLA_QK_PAD).astype(BF16)

    z = _matmul(h, w_in_p, F32, tm_target=1024, tn_target=768)
    a = _pool(z, pool_w.astype(BF16), pool_scale, seq)
    scale = (MLA_NOPE + MLA_ROPE) ** -0.5 * LOG2E
    q = _mla_q(z, pool_width // q_rank, q_norm, wq_pad, tabs, seq, scale)
    k, v = _mla_kv(z, (pool_width + q_rank) // kv_rank, (pool_width + q_rank + kv_rank) // LANES,
                   kv_norm, wkv_b.astype(BF16), tabs, seq)
    attn = _flash(q, k, _transpose_values(v, heads), seq, n_groups=heads // 2, n_heads=2, dq=MLA_QK_PAD, shared_kv=False,
                  shift_col=MLA_NOPE + MLA_SHIFT_LANE)
    w_out = w_out.astype(BF16)
    return _matmul_resid([a, attn], [w_out[:pool_width], w_out[pool_width:]], x, mods, seq,
                         gate_row=2, tm_target=1024, tn_target=512)


def _odd_layer(x, h, mods, seq, w_in, q_norm, k_norm, fnet_w, w_out, tabs):
    fnet_width = fnet_w.shape[0]
    qkv_width = w_in.shape[1] - fnet_width
    kd = qkv_width // (GQA_GROUP + 2)
    qd = GQA_GROUP * kd
    w_in = w_in.astype(BF16)
    perm = _half_split_perm(GQA_HEAD_DIM)
    d_in = w_in.shape[0]
    w_q = w_in[:, :qd].reshape(d_in, qd // GQA_HEAD_DIM, GQA_HEAD_DIM)[:, :, perm].reshape(d_in, qd)
    w_k = w_in[:, qd:qd + kd].reshape(d_in, kd // GQA_HEAD_DIM, GQA_HEAD_DIM)[:, :, perm].reshape(d_in, kd)
    w_kv = jnp.concatenate([w_k, w_in[:, qd + kd:qd + 2 * kd]], axis=1)

    q = _gqa_qk(h, w_q, q_norm[perm], tabs, seq, GQA_HEAD_DIM ** -0.5 * LOG2E)
    k, v = _gqa_kv(h, w_kv, k_norm[perm], tabs, seq)
    u = _matmul(h, w_in[:, qd + 2 * kd:], F32)
    attn = _flash(q, k, _transpose_values(v, kd // GQA_HEAD_DIM), seq, n_groups=kd // GQA_HEAD_DIM, n_heads=GQA_GROUP,
                  dq=GQA_HEAD_DIM, shared_kv=True, shift_col=GQA_HEAD_DIM)
    f = _fourier(u, seq)
    ff = _matmul(f, fnet_w.astype(BF16), BF16)
    w_out = w_out.astype(BF16)
    return _matmul_resid([attn, ff], [w_out[:qd], w_out[qd:]], x, mods, seq,
                         gate_row=2, tm_target=1024, tn_target=512)


def _ffn(x, mods, seq, norm_w, w_gate, w_up, w_down, layer):
    h = _normmod(x, norm_w, mods, seq, shift_row=3, scale_row=4)
    hid = _matmul_swiglu(h, w_gate, w_up, layer)
    return _matmul_resid([hid], [w_down], x, mods, seq,
                         gate_row=5, tm_target=512, tn_target=512, layer=layer)


def kernel(x_prompt, x_sample, c_prompt, c_sample, mod_w, mod_b, norm_mix, norm_ffn, e_w_in, e_pool_w, e_pool_scale, e_q_norm, e_wq_b, e_kv_norm, e_wkv_b, e_w_out, o_w_in, o_q_norm, o_k_norm, o_fnet_w, o_w_out, ffn_gate, ffn_up, ffn_down, final_norm):
    n_prompt, seq, d = x_prompt.shape
    n_sample = x_sample.shape[0]
    assert x_sample.shape[1] == seq and seq % GRID_W == 0 and seq % FFT_INNER == 0
    n_seq = n_prompt + n_sample
    depth = mod_w.shape[0]

    c = jnp.concatenate([c_prompt, c_sample], axis=0)
    c_pad = jnp.pad(c, ((0, -n_seq % SUBLANES), (0, 0)))
    mods_all = _modulation(c_pad, mod_w, mod_b).reshape(depth, c_pad.shape[0], 6, d)

    pos = jnp.arange(seq)
    ang_1d = pos.astype(F32)[:, None] * _rope_freqs(MLA_ROPE)[None, :]
    ang_row = (pos // GRID_W).astype(F32)[:, None] * _rope_freqs(GQA_HEAD_DIM // 2)[None, :]
    ang_col = (pos % GRID_W).astype(F32)[:, None] * _rope_freqs(GQA_HEAD_DIM // 2)[None, :]
    mla_tabs = _pair_tables(ang_1d, None)
    gqa_tabs = _pair_tables(ang_row, ang_col)

    ffn_gate_bf16, ffn_up_bf16, ffn_down_bf16 = (w.astype(BF16) for w in (ffn_gate, ffn_up, ffn_down))
    x, h = _stack_normmod(x_prompt.reshape(n_prompt * seq, d), x_sample.reshape(n_sample * seq, d),
                          norm_mix[0], mods_all[0], seq, shift_row=0, scale_row=1)
    for l in range(depth):
        mods = mods_all[l]
        i = l // 2
        if l > 0:
            h = _normmod(x, norm_mix[l], mods, seq, shift_row=0, scale_row=1)
        if l % 2 == 0:
            x = _even_layer(x, h, mods, seq, e_w_in[i], e_pool_w[i], e_pool_scale[i], e_q_norm[i],
                            e_wq_b[i], e_kv_norm[i], e_wkv_b[i], e_w_out[i], mla_tabs)
        else:
            x = _odd_layer(x, h, mods, seq, o_w_in[i], o_q_norm[i], o_k_norm[i], o_fnet_w[i],
                           o_w_out[i], gqa_tabs)
        x = _ffn(x, mods, seq, norm_ffn[l], ffn_gate_bf16, ffn_up_bf16, ffn_down_bf16, l)

    y_prompt = _final_norm(x, final_norm, seq, 0, n_prompt).reshape(n_prompt, seq, d)
    y_sample = _final_norm(x, final_norm, seq, n_prompt, n_sample).reshape(n_sample, seq, d)
    return (y_prompt, y_sample)
```
